```python
import numpy as np
import jax
import jax.numpy as jnp
from jax import lax

D_MODEL = 2048
BATCH = 1
SEQ = 8192
DEPTH = 4

POOL_WINDOWS = (2, 4, 8, 16)
N_POOL_GROUPS = 4
POOL_WIDTH = D_MODEL // 2
POOL_GROUP = POOL_WIDTH // N_POOL_GROUPS
HEAD_DIM = 128
ATTN_WIDTH = D_MODEL
N_HEADS = ATTN_WIDTH // HEAD_DIM
MOBA_BLOCK = 256
MOBA_TOPK = 3
Q_CHUNK = 32
IN_WIDTH = POOL_WIDTH + 3 * ATTN_WIDTH + 2 * D_MODEL
SPLITS = tuple(int(s) for s in np.cumsum([POOL_WIDTH, ATTN_WIDTH, ATTN_WIDTH, ATTN_WIDTH, D_MODEL]))
D_FF = 256 * ((8 * D_MODEL // 3 + 255) // 256)
N_EXPERTS = 8
MOE_TOPK = 2
D_FF_EXPERT = 7 * D_MODEL // 2
ROW_BLOCK = 512
N_DENSE = (DEPTH + 1) // 2
N_MOE = DEPTH // 2
ALPHA = (2 * DEPTH) ** 0.25
BETA = (8 * DEPTH) ** -0.25
LN_EPS = 1e-5

kernel_name = 'hybrid_pool_moba_moe_deepnorm'


def layer_norm(x, g, b):
    xf = x.astype(jnp.float32)
    mu = jnp.mean(xf, axis=-1, keepdims=True)
    var = jnp.mean(jnp.square(xf - mu), axis=-1, keepdims=True)
    return ((xf - mu) * lax.rsqrt(var + LN_EPS) * g + b).astype(x.dtype)


def pool_mixer(u, pool_w, pool_scale):
    B, S, _ = u.shape
    uf = u.astype(jnp.float32).reshape(B, S, N_POOL_GROUPS, POOL_GROUP)
    cs = jnp.cumsum(uf, axis=1)
    t = jnp.arange(S)
    outs = []
    for g, w in enumerate(POOL_WINDOWS):
        c = cs[:, :, g]
        lag = jnp.pad(c, ((0, 0), (w, 0), (0, 0)))[:, :S]
        cnt = jnp.minimum(t + 1, w).astype(jnp.float32)[None, :, None]
        outs.append((c - lag) / cnt - uf[:, :, g])
    pooled = jnp.stack(outs, axis=2).astype(u.dtype)
    y = jnp.einsum('bsgc,gcd->bsgd', pooled, pool_w)
    return y.reshape(B, S, POOL_WIDTH) * pool_scale


def moba_attention(q, k, v):
    B, H, S, Dh = q.shape
    nb = -(-S // MOBA_BLOCK)
    pad = nb * MOBA_BLOCK - S
    kb = jnp.pad(k, ((0, 0), (0, 0), (0, pad), (0, 0))).reshape(B, H, nb, MOBA_BLOCK, Dh)
    vb = jnp.pad(v, ((0, 0), (0, 0), (0, pad), (0, 0))).reshape(B, H, nb, MOBA_BLOCK, Dh)
    k_mean = jnp.sum(kb.astype(jnp.float32), axis=3) / MOBA_BLOCK
    n_sel = min(MOBA_TOPK, nb)
    scale = HEAD_DIM ** -0.5
    nq = S // Q_CHUNK
    q_chunks = q.reshape(B, H, nq, Q_CHUNK, Dh).transpose(2, 0, 1, 3, 4)
    gather_blocks = jax.vmap(jax.vmap(lambda blocks, idx: blocks[idx]))

    def one_chunk(args):
        c, qc = args
        t0 = c * Q_CHUNK
        blk = t0 // MOBA_BLOCK
        q_pos = t0 + jnp.arange(Q_CHUNK)
        gate = jnp.einsum('bhqd,bhnd->bhqn', qc.astype(jnp.float32), k_mean)
        gate = jnp.where(jnp.arange(nb) < blk, gate, -jnp.inf)
        _, sel = lax.top_k(gate, n_sel)
        sel_valid = jnp.arange(n_sel) < blk
        k_sel = gather_blocks(kb, sel)
        v_sel = gather_blocks(vb, sel)
        s_sel = jnp.einsum('bhqd,bhqrkd->bhqrk', qc, k_sel).astype(jnp.float32) * scale
        s_sel = jnp.where(sel_valid[:, None], s_sel, -jnp.inf)
        k_own = lax.dynamic_index_in_dim(kb, blk, axis=2, keepdims=False)
        v_own = lax.dynamic_index_in_dim(vb, blk, axis=2, keepdims=False)
        s_own = jnp.einsum('bhqd,bhkd->bhqk', qc, k_own).astype(jnp.float32) * scale
        k_pos = blk * MOBA_BLOCK + jnp.arange(MOBA_BLOCK)
        s_own = jnp.where(k_pos[None, :] <= q_pos[:, None], s_own, -jnp.inf)
        s = jnp.concatenate([s_sel.reshape(B, H, Q_CHUNK, n_sel * MOBA_BLOCK), s_own], axis=-1)
        p = jax.nn.softmax(s, axis=-1).astype(v.dtype)
        p_sel = p[..., :n_sel * MOBA_BLOCK].reshape(B, H, Q_CHUNK, n_sel, MOBA_BLOCK)
        p_own = p[..., n_sel * MOBA_BLOCK:]
        return (jnp.einsum('bhqrk,bhqrkd->bhqd', p_sel, v_sel)
                + jnp.einsum('bhqk,bhkd->bhqd', p_own, v_own))

    out = lax.map(one_chunk, (jnp.arange(nq), q_chunks))
    return out.transpose(1, 2, 0, 3, 4).reshape(B, H, S, Dh)


def hybrid_mixer(x, w_in, pool_w, pool_scale, w_up_pool, w_up_attn, w_o):
    B, S, _ = x.shape
    z = x @ w_in
    u, q, k, v, g_pool, g_attn = jnp.split(z, SPLITS, axis=-1)
    y_pool = pool_mixer(u, pool_w, pool_scale) @ w_up_pool
    to_heads = lambda t: t.reshape(B, S, N_HEADS, HEAD_DIM).transpose(0, 2, 1, 3)
    o = moba_attention(to_heads(q), to_heads(k), to_heads(v))
    y_attn = o.transpose(0, 2, 1, 3).reshape(B, S, ATTN_WIDTH) @ w_up_attn
    m = jax.nn.sigmoid(g_pool) * y_pool + jax.nn.sigmoid(g_attn) * y_attn
    return m @ w_o


def swiglu(h, w_gate, w_up, w_down):
    return (jax.nn.silu(h @ w_gate) * (h @ w_up)) @ w_down


def moe_swiglu(h, router_w, w_gate, w_up, w_down):
    T, D = h.shape
    logits = (h @ router_w).astype(jnp.float32)
    top_logit, top_e = lax.top_k(logits, MOE_TOPK)
    top_w = jax.nn.softmax(top_logit, axis=-1)
    n_assign = T * MOE_TOPK
    flat_e = top_e.reshape(-1)
    flat_tok = jnp.arange(n_assign, dtype=jnp.int32) // MOE_TOPK
    flat_w = top_w.reshape(-1)
    order = jnp.argsort(flat_e)
    se, stok, sw = flat_e[order], flat_tok[order], flat_w[order]
    counts = jnp.bincount(flat_e, length=N_EXPERTS)
    padded = (counts + ROW_BLOCK - 1) // ROW_BLOCK * ROW_BLOCK
    start = jnp.cumsum(counts) - counts
    ends = jnp.cumsum(padded)
    pstart = ends - padded
    dest = pstart[se] + jnp.arange(n_assign) - start[se]
    n_rows = (-(-n_assign // ROW_BLOCK) + N_EXPERTS) * ROW_BLOCK
    n_blocks = n_rows // ROW_BLOCK
    row_tok = jnp.full((n_rows,), T, dtype=jnp.int32).at[dest].set(stok)
    row_w = jnp.zeros((n_rows,), jnp.float32).at[dest].set(sw)
    block_e = jnp.minimum(jnp.searchsorted(ends, jnp.arange(n_blocks) * ROW_BLOCK, side='right'),
                          N_EXPERTS - 1)
    h_pad = jnp.concatenate([h, jnp.zeros((1, D), h.dtype)], axis=0)

    def expert_block(args):
        e, toks = args
        xb = h_pad[toks]
        return swiglu(xb, w_gate[e], w_up[e], w_down[e])

    y = lax.map(expert_block, (block_e, row_tok.reshape(n_blocks, ROW_BLOCK)))
    y = y.reshape(n_rows, D) * row_w[:, None].astype(y.dtype)
    out = jnp.zeros((T + 1, D), y.dtype).at[row_tok].add(y)
    return out[:T]


def setup_inputs(seed: int = 0) -> dict:
    key = jax.random.key(seed)
    ks = jax.random.split(key, 20)
    D = D_MODEL
    nrm = lambda k, shape, s: jax.random.normal(k, shape, jnp.float32) * s
    return {
        'x': nrm(ks[0], (BATCH, SEQ, D), 1.0),
        'ln_in_g': 1.0 + nrm(ks[1], (D,), 0.05),
        'ln_in_b': nrm(ks[2], (D,), 0.02),
        'w_in': nrm(ks[3], (DEPTH, D, IN_WIDTH), D ** -0.5),
        'pool_w': nrm(ks[4], (DEPTH, N_POOL_GROUPS, POOL_GROUP, POOL_GROUP), POOL_GROUP ** -0.5),
        'pool_scale': 1.0 + nrm(ks[5], (DEPTH, POOL_WIDTH), 0.1),
        'w_up_pool': nrm(ks[6], (DEPTH, POOL_WIDTH, D), POOL_WIDTH ** -0.5),
        'w_up_attn': nrm(ks[7], (DEPTH, ATTN_WIDTH, D), ATTN_WIDTH ** -0.5),
        'w_o': nrm(ks[8], (DEPTH, D, D), BETA * D ** -0.5),
        'ln_mix_g': 1.0 + nrm(ks[9], (DEPTH, D), 0.05),
        'ln_mix_b': nrm(ks[10], (DEPTH, D), 0.02),
        'ffn_w_gate': nrm(ks[11], (N_DENSE, D, D_FF), D ** -0.5),
        'ffn_w_up': nrm(ks[12], (N_DENSE, D, D_FF), D ** -0.5),
        'ffn_w_down': nrm(ks[13], (N_DENSE, D_FF, D), BETA * D_FF ** -0.5),
        'moe_router': nrm(ks[14], (N_MOE, D, N_EXPERTS), D ** -0.5),
        'moe_w_gate': nrm(ks[15], (N_MOE, N_EXPERTS, D, D_FF_EXPERT), D ** -0.5),
        'moe_w_up': nrm(ks[16], (N_MOE, N_EXPERTS, D, D_FF_EXPERT), D ** -0.5),
        'moe_w_down': nrm(ks[17], (N_MOE, N_EXPERTS, D_FF_EXPERT, D), BETA * D_FF_EXPERT ** -0.5),
        'ln_ffn_g': 1.0 + nrm(ks[18], (DEPTH, D), 0.05),
        'ln_ffn_b': nrm(ks[19], (DEPTH, D), 0.02),
    }


def reference(x, ln_in_g, ln_in_b, w_in, pool_w, pool_scale, w_up_pool, w_up_attn, w_o,
              ln_mix_g, ln_mix_b, ffn_w_gate, ffn_w_up, ffn_w_down, moe_router, moe_w_gate,
              moe_w_up, moe_w_down, ln_ffn_g, ln_ffn_b):
    B, S, D = x.shape
    x = layer_norm(x, ln_in_g, ln_in_b)
    for l in range(DEPTH):
        mix = hybrid_mixer(x, w_in[l], pool_w[l], pool_scale[l], w_up_pool[l], w_up_attn[l], w_o[l])
        x = layer_norm(ALPHA * x + mix, ln_mix_g[l], ln_mix_b[l])
        i = l // 2
        if l % 2 == 0:
            f = swiglu(x, ffn_w_gate[i], ffn_w_up[i], ffn_w_down[i])
        else:
            f = moe_swiglu(x.reshape(B * S, D), moe_router[i], moe_w_gate[i], moe_w_up[i],
                           moe_w_down[i]).reshape(B, S, D)
        x = layer_norm(ALPHA * x + f, ln_ffn_g[l], ln_ffn_b[l])
    return x
```

```python
import functools

import jax
import jax.numpy as jnp
import numpy as np
from jax import lax
from jax.experimental import pallas as pl
from jax.experimental.pallas import tpu as pltpu

F32 = jnp.float32
BF16 = jnp.bfloat16

HEAD_DIM = 128
MOBA_BLOCK = 256
MOBA_TOPK = 3
POOL_WINDOWS = (2, 4, 8, 16)
POOL_HALO = 16
MOE_TOPK = 2
LN_EPS = 1e-5
LANES = 128
VMEM_LIMIT = 56 * 1024 * 1024

EXPERT_CHUNK = 1024
EXPERT_SUB = 512
GATHER_ROWS = 256


def _tile(n, pref, mult=LANES):
    if n <= pref:
        return n
    t = (pref // mult) * mult
    while t > mult and n % t:
        t -= mult
    assert n % t == 0, (n, pref)
    return t


def _params(*sem):
    return pltpu.CompilerParams(dimension_semantics=sem, vmem_limit_bytes=VMEM_LIMIT)


def _layer_norm(v, g, b):
    mu = jnp.mean(v, axis=-1, keepdims=True)
    d = v - mu
    var = jnp.mean(d * d, axis=-1, keepdims=True)
    return d * lax.rsqrt(var + LN_EPS) * g + b


def _sigmoid(v):
    return 1.0 / (1.0 + jnp.exp(-v))


def _dot(a, b):
    return jnp.dot(a, b, preferred_element_type=F32)


def _dot_nt(a, b):
    return lax.dot_general(a, b, (((1,), (1,)), ((), ())), preferred_element_type=F32)


def _ln_kernel(x_ref, g_ref, b_ref, xf_ref, xb_ref):
    y = _layer_norm(x_ref[...], g_ref[...], b_ref[...])
    xf_ref[...] = y
    xb_ref[...] = y.astype(BF16)


def _ln_in(x, g, b):
    S, D = x.shape
    tm = _tile(S, 512, 8)
    row = pl.BlockSpec((tm, D), lambda i: (i, 0))
    vec = pl.BlockSpec((1, D), lambda i: (0, 0))
    return pl.pallas_call(
        _ln_kernel, name="ln_in",
        grid=(S // tm,),
        in_specs=[row, vec, vec],
        out_specs=[row, row],
        out_shape=[jax.ShapeDtypeStruct((S, D), F32), jax.ShapeDtypeStruct((S, D), BF16)],
        compiler_params=_params("parallel"),
    )(x, g.reshape(1, D), b.reshape(1, D))


def _add_ln_kernel(x_ref, f_ref, g_ref, b_ref, xf_ref, xb_ref, *, alpha):
    y = _layer_norm(alpha * x_ref[...] + f_ref[...], g_ref[...], b_ref[...])
    xf_ref[...] = y
    xb_ref[...] = y.astype(BF16)


def _add_ln(x, f, g, b, alpha):
    S, D = x.shape
    tm = _tile(S, 512, 8)
    row = pl.BlockSpec((tm, D), lambda i: (i, 0))
    vec = pl.BlockSpec((1, D), lambda i: (0, 0))
    return pl.pallas_call(
        functools.partial(_add_ln_kernel, alpha=alpha), name="add_ln",
        grid=(S // tm,),
        in_specs=[row, row, vec, vec],
        out_specs=[row, row],
        out_shape=[jax.ShapeDtypeStruct((S, D), F32), jax.ShapeDtypeStruct((S, D), BF16)],
        compiler_params=_params("parallel"),
    )(x, f, g.reshape(1, D), b.reshape(1, D))


def _proj_kernel(a_ref, b_ref, o_ref, *, mode, q_tiles, q_scale):
    acc = _dot(a_ref[...], b_ref[0])
    if mode == "qkv":
        acc = acc * jnp.where(pl.program_id(1) < q_tiles, q_scale, 1.0).astype(F32)
    elif mode == "sigmoid":
        acc = _sigmoid(acc)
    o_ref[...] = acc.astype(o_ref.dtype)


def _proj(xb, w_in, layer, col0, width, out_dtype, mode, q_width=0):
    S, D = xb.shape
    tm = _tile(S, 1024, 8)
    tn = _tile(int(np.gcd.reduce([width, col0 or width, q_width or width])), 1024)
    assert col0 % tn == 0 and width % tn == 0 and q_width % tn == 0
    off = col0 // tn
    kern = functools.partial(_proj_kernel, mode=mode, q_tiles=q_width // tn,
                             q_scale=HEAD_DIM ** -0.5)
    return pl.pallas_call(
        kern, name="proj_" + mode,
        grid=(S // tm, width // tn),
        in_specs=[pl.BlockSpec((tm, D), lambda i, n: (i, 0)),
                  pl.BlockSpec((1, D, tn), lambda i, n: (layer, 0, off + n))],
        out_specs=pl.BlockSpec((tm, tn), lambda i, n: (i, n)),
        out_shape=jax.ShapeDtypeStruct((S, width), out_dtype),
        compiler_params=_params("parallel", "arbitrary"),
    )(xb, w_in)


def _pool_kernel(u_ref, uprev_ref, pw_ref, sc_ref, o_ref, ext_ref, *, tp, group):
    i = pl.program_id(0)
    halo = uprev_ref[...]
    ext_ref[0:POOL_HALO, :] = jnp.where(i > 0, halo, jnp.zeros_like(halo))
    ext_ref[POOL_HALO:, :] = u_ref[...]
    t = i * tp + lax.broadcasted_iota(jnp.int32, (tp, 1), 0)
    for g, w in enumerate(POOL_WINDOWS):
        cols = slice(g * group, (g + 1) * group)
        s = ext_ref[:, cols]
        k = 1
        while k < w:
            s = s + pltpu.roll(s, k, 0)
            k *= 2
        cnt = jnp.minimum(t + 1, w).astype(F32)
        cur = u_ref[:, cols]
        pooled = s[POOL_HALO:, :] / cnt - cur
        y = _dot(pooled.astype(BF16), pw_ref[g]) * sc_ref[:, cols]
        o_ref[:, cols] = y.astype(o_ref.dtype)


def _pool(u, pool_w_b, pool_scale):
    S, PW = u.shape
    G, C, _ = pool_w_b.shape
    assert G == len(POOL_WINDOWS) and G * C == PW
    tp = _tile(S, 512, POOL_HALO)
    per = tp // POOL_HALO
    return pl.pallas_call(
        functools.partial(_pool_kernel, tp=tp, group=C), name="pool",
        grid=(S // tp,),
        in_specs=[pl.BlockSpec((tp, PW), lambda i: (i, 0)),
                  pl.BlockSpec((POOL_HALO, PW), lambda i: (jnp.maximum(i * per - 1, 0), 0)),
                  pl.BlockSpec((G, C, C), lambda i: (0, 0, 0)),
                  pl.BlockSpec((1, PW), lambda i: (0, 0))],
        out_specs=pl.BlockSpec((tp, PW), lambda i: (i, 0)),
        out_shape=jax.ShapeDtypeStruct((S, PW), BF16),
        scratch_shapes=[pltpu.VMEM((tp + POOL_HALO, PW), F32)],
        compiler_params=_params("parallel"),
    )(u, u, pool_w_b, pool_scale.reshape(1, PW))


def _moba_kernel(q_ref, k_ref, v_ref, o_ref, kmean_ref, *, nb):
    i = pl.program_id(1)
    B = MOBA_BLOCK

    @pl.when(i == 0)
    def _():
        kf = k_ref[...].astype(F32).reshape(nb, B, HEAD_DIM)
        kmean_ref[...] = jnp.sum(kf, axis=1) / B

    q = q_ref[...]
    km = kmean_ref[...]
    km_hi = km.astype(BF16)
    km_lo = (km - km_hi.astype(F32)).astype(BF16)
    gate = _dot_nt(q, km_hi) + _dot_nt(q, km_lo)

    col = lax.broadcasted_iota(jnp.int32, (B, nb), 1)
    neg = jnp.float32(-jnp.inf)
    g = jnp.where(col < i, gate, neg)
    bias = jnp.full((B, nb), neg, F32)
    for _ in range(MOBA_TOPK):
        m = jnp.max(g, axis=1, keepdims=True)
        idx = jnp.min(jnp.where(g == m, col, nb), axis=1, keepdims=True)
        hit = col == idx
        bias = jnp.where(hit & (m > neg), 0.0, bias)
        g = jnp.where(hit, neg, g)

    row0 = pl.multiple_of(i * B, B)
    s = _dot_nt(q, k_ref[pl.ds(row0, B), :])
    r_id = lax.broadcasted_iota(jnp.int32, (B, B), 0)
    c_id = lax.broadcasted_iota(jnp.int32, (B, B), 1)
    s = jnp.where(c_id <= r_id, s, neg)
    m0 = jnp.max(s, axis=1, keepdims=True)
    p = jnp.exp(s - m0)
    l0 = jnp.sum(p, axis=1, keepdims=True)
    acc0 = _dot(p.astype(BF16), v_ref[pl.ds(row0, B), :])

    def body(j, carry):
        m_run, l_run, acc = carry
        rj = pl.multiple_of(j * B, B)
        bj = jnp.sum(jnp.where(col == j, bias, 0.0), axis=1, keepdims=True)
        sj = _dot_nt(q, k_ref[pl.ds(rj, B), :]) + bj
        m_new = jnp.maximum(m_run, jnp.max(sj, axis=1, keepdims=True))
        a = jnp.exp(m_run - m_new)
        pj = jnp.exp(sj - m_new)
        l_new = a * l_run + jnp.sum(pj, axis=1, keepdims=True)
        acc_new = a * acc + _dot(pj.astype(BF16), v_ref[pl.ds(rj, B), :])
        return m_new, l_new, acc_new

    _, l_fin, acc_fin = lax.fori_loop(0, i, body, (m0, l0, acc0))
    o_ref[...] = (acc_fin / l_fin).astype(o_ref.dtype)


def _moba(qkv, n_heads):
    S = qkv.shape[0]
    assert S % MOBA_BLOCK == 0
    nb = S // MOBA_BLOCK
    H = n_heads
    return pl.pallas_call(
        functools.partial(_moba_kernel, nb=nb), name="moba",
        grid=(H, nb),
        in_specs=[pl.BlockSpec((MOBA_BLOCK, HEAD_DIM), lambda h, i: (i, h)),
                  pl.BlockSpec((S, HEAD_DIM), lambda h, i: (0, H + h)),
                  pl.BlockSpec((S, HEAD_DIM), lambda h, i: (0, 2 * H + h))],
        out_specs=pl.BlockSpec((MOBA_BLOCK, HEAD_DIM), lambda h, i: (i, h)),
        out_shape=jax.ShapeDtypeStruct((S, H * HEAD_DIM), BF16),
        scratch_shapes=[pltpu.VMEM((nb, HEAD_DIM), F32)],
        compiler_params=_params("parallel", "arbitrary"),
    )(qkv, qkv, qkv)


def _mix_up_kernel(p_ref, a_ref, wp_ref, wa_ref, gp_ref, ga_ref, o_ref):
    y_pool = _dot(p_ref[...], wp_ref[...])
    y_attn = _dot(a_ref[...], wa_ref[...])
    o_ref[...] = (gp_ref[...] * y_pool + ga_ref[...] * y_attn).astype(o_ref.dtype)


def _mix_up(p, o, w_up_pool_b, w_up_attn_b, gates):
    S, PW = p.shape
    AW = o.shape[1]
    D = w_up_pool_b.shape[1]
    tm = _tile(S, 512, 8)
    tn = _tile(D, 1024)
    nt = D // tn
    return pl.pallas_call(
        _mix_up_kernel, name="mix_up",
        grid=(S // tm, nt),
        in_specs=[pl.BlockSpec((tm, PW), lambda i, n: (i, 0)),
                  pl.BlockSpec((tm, AW), lambda i, n: (i, 0)),
                  pl.BlockSpec((PW, tn), lambda i, n: (0, n)),
                  pl.BlockSpec((AW, tn), lambda i, n: (0, n)),
                  pl.BlockSpec((tm, tn), lambda i, n: (i, n)),
                  pl.BlockSpec((tm, tn), lambda i, n: (i, nt + n))],
        out_specs=pl.BlockSpec((tm, tn), lambda i, n: (i, n)),
        out_shape=jax.ShapeDtypeStruct((S, D), BF16),
        compiler_params=_params("parallel", "arbitrary"),
    )(p, o, w_up_pool_b, w_up_attn_b, gates, gates)


def _mix_out_kernel(m_ref, w_ref, x_ref, g_ref, b_ref, xf_ref, xb_ref, *, alpha):
    mix = _dot(m_ref[...], w_ref[...])
    y = _layer_norm(alpha * x_ref[...] + mix, g_ref[...], b_ref[...])
    xf_ref[...] = y
    xb_ref[...] = y.astype(BF16)


def _mix_out(m, w_o_b, x, g, b, alpha):
    S, D = x.shape
    tm = _tile(S, 256, 8)
    row = lambda dt: pl.BlockSpec((tm, D), lambda i: (i, 0))
    vec = pl.BlockSpec((1, D), lambda i: (0, 0))
    return pl.pallas_call(
        functools.partial(_mix_out_kernel, alpha=alpha), name="mix_out",
        grid=(S // tm,),
        in_specs=[row(BF16), pl.BlockSpec((D, D), lambda i: (0, 0)), row(F32), vec, vec],
        out_specs=[row(F32), row(BF16)],
        out_shape=[jax.ShapeDtypeStruct((S, D), F32), jax.ShapeDtypeStruct((S, D), BF16)],
        compiler_params=_params("parallel"),
    )(m, w_o_b, x, g.reshape(1, D), b.reshape(1, D))


def _swiglu_kernel(nvalid_ref, eid_ref, x_ref, wg_ref, wu_ref, wd_ref, o_ref, *, sub):
    c = pl.program_id(0)
    f = pl.program_id(1)
    nv = nvalid_ref[c]
    rows = x_ref.shape[0]
    for s0 in range(0, rows, sub):
        rs = slice(s0, s0 + sub)

        @pl.when(nv > s0)
        def _():
            xs = x_ref[rs, :]
            gate = _dot(xs, wg_ref[0].astype(BF16))
            up = _dot(xs, wu_ref[0].astype(BF16))
            h = (gate * _sigmoid(gate) * up).astype(BF16)
            y = _dot(h, wd_ref[0].astype(BF16))

            @pl.when(f == 0)
            def _():
                o_ref[rs, :] = y

            @pl.when(f > 0)
            def _():
                o_ref[rs, :] += y

        @pl.when((nv <= s0) & (f == 0))
        def _():
            o_ref[rs, :] = jnp.zeros((sub, o_ref.shape[1]), F32)


def _swiglu(xrows, w_gate, w_up, w_down, nvalid, eid, w_base, chunk, tf_pref):
    R, D = xrows.shape
    F = w_gate.shape[-1]
    tf = _tile(F, tf_pref)
    sub = min(EXPERT_SUB, chunk)
    assert R % chunk == 0 and chunk % sub == 0

    def w_in_map(c, f, nv, e):
        return (w_base + e[c], 0, jnp.where(nv[c] > 0, f, 0))

    def w_out_map(c, f, nv, e):
        return (w_base + e[c], jnp.where(nv[c] > 0, f, 0), 0)

    grid_spec = pltpu.PrefetchScalarGridSpec(
        num_scalar_prefetch=2,
        grid=(R // chunk, F // tf),
        in_specs=[pl.BlockSpec((chunk, D), lambda c, f, nv, e: (c, 0)),
                  pl.BlockSpec((1, D, tf), w_in_map),
                  pl.BlockSpec((1, D, tf), w_in_map),
                  pl.BlockSpec((1, tf, D), w_out_map)],
        out_specs=pl.BlockSpec((chunk, D), lambda c, f, nv, e: (c, 0)),
    )
    return pl.pallas_call(
        functools.partial(_swiglu_kernel, sub=sub), name="swiglu",
        grid_spec=grid_spec,
        out_shape=jax.ShapeDtypeStruct((R, D), F32),
        compiler_params=_params("parallel", "arbitrary"),
    )(nvalid, eid, xrows, w_gate, w_up, w_down)


def _router_kernel(x_ref, rh_ref, rl_ref, o_ref):
    x = x_ref[...]
    xh = x.astype(BF16)
    xl = (x - xh.astype(F32)).astype(BF16)
    o_ref[...] = _dot(xh, rh_ref[...]) + _dot(xl, rh_ref[...]) + _dot(xh, rl_ref[...])


def _router(x, router_w):
    S, D = x.shape
    E = router_w.shape[1]
    wpad = jnp.pad(router_w, ((0, 0), (0, LANES - E)))
    rh = wpad.astype(BF16)
    rl = (wpad - rh.astype(F32)).astype(BF16)
    tm = _tile(S, 512, 8)
    wspec = pl.BlockSpec((D, LANES), lambda i: (0, 0))
    logits = pl.pallas_call(
        _router_kernel, name="router",
        grid=(S // tm,),
        in_specs=[pl.BlockSpec((tm, D), lambda i: (i, 0)), wspec, wspec],
        out_specs=pl.BlockSpec((tm, LANES), lambda i: (i, 0)),
        out_shape=jax.ShapeDtypeStruct((S, LANES), F32),
        compiler_params=_params("parallel"),
    )(x, rh, rl)
    return logits[:, :E]


def _gather_kernel(tok_ref, valid_ref, x_hbm, o_ref, buf_ref, sem, *, rows):
    b = pl.program_id(0)
    base = b * rows

    @pl.when(valid_ref[b] > 0)
    def _():
        def issue(r, carry):
            tok = tok_ref[base + r]
            pltpu.make_async_copy(x_hbm.at[pl.ds(tok, 1), :], buf_ref.at[pl.ds(r, 1), :], sem).start()
            return carry

        lax.fori_loop(0, rows, issue, 0)
        pltpu.make_async_copy(x_hbm.at[pl.ds(0, rows), :], buf_ref, sem).wait()
        o_ref[...] = buf_ref[...].astype(o_ref.dtype)

    @pl.when(valid_ref[b] == 0)
    def _():
        o_ref[...] = jnp.zeros(o_ref.shape, o_ref.dtype)


def _gather_rows(x, row_tok, blk_valid, n_rows):
    S, D = x.shape
    rows = GATHER_ROWS
    grid_spec = pltpu.PrefetchScalarGridSpec(
        num_scalar_prefetch=2,
        grid=(n_rows // rows,),
        in_specs=[pl.BlockSpec(memory_space=pl.ANY)],
        out_specs=pl.BlockSpec((rows, D), lambda b, t, v: (b, 0)),
        scratch_shapes=[pltpu.VMEM((rows, D), F32), pltpu.SemaphoreType.DMA(())],
    )
    return pl.pallas_call(
        functools.partial(_gather_kernel, rows=rows), name="gather_rows",
        grid_spec=grid_spec,
        out_shape=jax.ShapeDtypeStruct((n_rows, D), BF16),
        compiler_params=_params("arbitrary"),
    )(row_tok, blk_valid, x)


def _combine_kernel(pos_ref, y_hbm, w_ref, x_ref, g_ref, b_ref, xf_ref, xb_ref, buf_ref, sem,
                    *, rows, alpha):
    base = pl.program_id(0) * rows * MOE_TOPK

    def issue(r, carry):
        for k in range(MOE_TOPK):
            src = pos_ref[base + r * MOE_TOPK + k]
            pltpu.make_async_copy(y_hbm.at[pl.ds(src, 1), :], buf_ref.at[k, pl.ds(r, 1), :], sem).start()
        return carry

    lax.fori_loop(0, rows, issue, 0)
    for k in range(MOE_TOPK):
        pltpu.make_async_copy(y_hbm.at[pl.ds(0, rows), :], buf_ref.at[k], sem).wait()
    w = w_ref[...]
    f = w[:, 0:1] * buf_ref[0] + w[:, 1:2] * buf_ref[1]
    y = _layer_norm(alpha * x_ref[...] + f, g_ref[...], b_ref[...])
    xf_ref[...] = y
    xb_ref[...] = y.astype(BF16)


def _combine_ln(y, pos, top_w, x, g, b, alpha):
    S, D = x.shape
    rows = _tile(S, GATHER_ROWS, 8)
    row = pl.BlockSpec((rows, D), lambda i, p: (i, 0))
    vec = pl.BlockSpec((1, D), lambda i, p: (0, 0))
    grid_spec = pltpu.PrefetchScalarGridSpec(
        num_scalar_prefetch=1,
        grid=(S // rows,),
        in_specs=[pl.BlockSpec(memory_space=pl.ANY),
                  pl.BlockSpec((rows, MOE_TOPK), lambda i, p: (i, 0)),
                  row, vec, vec],
        out_specs=[row, row],
        scratch_shapes=[pltpu.VMEM((MOE_TOPK, rows, D), F32), pltpu.SemaphoreType.DMA(())],
    )
    return pl.pallas_call(
        functools.partial(_combine_kernel, rows=rows, alpha=alpha), name="combine_ln",
        grid_spec=grid_spec,
        out_shape=[jax.ShapeDtypeStruct((S, D), F32), jax.ShapeDtypeStruct((S, D), BF16)],
        compiler_params=_params("arbitrary"),
    )(pos, y, top_w, x, g.reshape(1, D), b.reshape(1, D))


def _route(logits, chunk):
    S, E = logits.shape
    top_logit, top_e = lax.top_k(logits, MOE_TOPK)
    top_w = jax.nn.softmax(top_logit, axis=-1)
    n_assign = S * MOE_TOPK
    flat_e = top_e.reshape(-1).astype(jnp.int32)
    onehot = (flat_e[:, None] == jnp.arange(E, dtype=jnp.int32)[None, :]).astype(jnp.int32)
    running = jnp.cumsum(onehot, axis=0)
    rank = jnp.sum(running * onehot, axis=1) - 1
    counts = running[-1]
    padded = (counts + chunk - 1) // chunk * chunk
    ends = jnp.cumsum(padded)
    pstart = ends - padded
    dest = pstart[flat_e] + rank
    n_rows = (-(-n_assign // chunk) + E) * chunk
    n_chunks = n_rows // chunk
    flat_tok = jnp.arange(n_assign, dtype=jnp.int32) // MOE_TOPK
    row_tok = jnp.zeros((n_rows,), jnp.int32).at[dest].set(flat_tok)
    chunk_start = jnp.arange(n_chunks, dtype=jnp.int32) * chunk
    eid = jnp.minimum(jnp.searchsorted(ends, chunk_start, side="right"), E - 1).astype(jnp.int32)
    nvalid = jnp.clip(counts[eid] - (chunk_start - pstart[eid]), 0, chunk).astype(jnp.int32)
    per = chunk // GATHER_ROWS
    blk = jnp.arange(n_rows // GATHER_ROWS, dtype=jnp.int32)
    blk_valid = (nvalid[blk // per] > (blk % per) * GATHER_ROWS).astype(jnp.int32)
    return top_w, dest.astype(jnp.int32), row_tok, eid, nvalid, blk_valid, n_rows


def kernel(x, ln_in_g, ln_in_b, w_in, pool_w, pool_scale, w_up_pool, w_up_attn, w_o, ln_mix_g, ln_mix_b, ffn_w_gate, ffn_w_up, ffn_w_down, moe_router, moe_w_gate, moe_w_up, moe_w_down, ln_ffn_g, ln_ffn_b):
    B, S, D = x.shape
    assert B == 1
    depth = w_in.shape[0]
    PW = w_up_pool.shape[1]
    AW = w_up_attn.shape[1]
    n_heads = AW // HEAD_DIM
    n_exp = moe_router.shape[-1]
    alpha = float((2 * depth) ** 0.25)

    w_in_b = w_in.astype(BF16)
    pool_w_b = pool_w.astype(BF16)
    w_up_pool_b = w_up_pool.astype(BF16)
    w_up_attn_b = w_up_attn.astype(BF16)
    w_o_b = w_o.astype(BF16)
    ffn_gate_b = ffn_w_gate.astype(BF16)
    ffn_up_b = ffn_w_up.astype(BF16)
    ffn_down_b = ffn_w_down.astype(BF16)
    moe_gate = moe_w_gate.reshape((-1,) + moe_w_gate.shape[2:])
    moe_up = moe_w_up.reshape((-1,) + moe_w_up.shape[2:])
    moe_down = moe_w_down.reshape((-1,) + moe_w_down.shape[2:])

    chunk = min(EXPERT_CHUNK, S)
    dense_nvalid = jnp.full((S // chunk,), chunk, jnp.int32)
    dense_eid = jnp.zeros((S // chunk,), jnp.int32)

    xf, xb = _ln_in(x.reshape(S, D), ln_in_g, ln_in_b)
    for l in range(depth):
        u = _proj(xb, w_in_b, l, 0, PW, F32, "plain")
        qkv = _proj(xb, w_in_b, l, PW, 3 * AW, BF16, "qkv", q_width=AW)
        gates = _proj(xb, w_in_b, l, PW + 3 * AW, 2 * D, F32, "sigmoid")
        p = _pool(u, pool_w_b[l], pool_scale[l])
        o = _moba(qkv, n_heads)
        m = _mix_up(p, o, w_up_pool_b[l], w_up_attn_b[l], gates)
        xf, xb = _mix_out(m, w_o_b[l], xf, ln_mix_g[l], ln_mix_b[l], alpha)
        i = l // 2
        if l % 2 == 0:
            f = _swiglu(xb, ffn_gate_b, ffn_up_b, ffn_down_b, dense_nvalid, dense_eid, i, chunk, 512)
            xf, xb = _add_ln(xf, f, ln_ffn_g[l], ln_ffn_b[l], alpha)
        else:
            logits = _router(xf, moe_router[i])
            top_w, dest, row_tok, eid, nvalid, blk_valid, n_rows = _route(logits, chunk)
            xg = _gather_rows(xf, row_tok, blk_valid, n_rows)
            y = _swiglu(xg, moe_gate, moe_up, moe_down, nvalid, eid, i * n_exp, chunk, 256)
            xf, xb = _combine_ln(y, dest, top_w, xf, ln_ffn_g[l], ln_ffn_b[l], alpha)
    return xf.reshape(B, S, D)
```

```python
import functools

import jax
import jax.numpy as jnp
import numpy as np
from jax import lax
from jax.experimental import pallas as pl
from jax.experimental.pallas import tpu as pltpu

F32 = jnp.float32
BF16 = jnp.bfloat16

HEAD_DIM = 128
MOBA_BLOCK = 256
MOBA_TOPK = 3
MOBA_GROUP = 4
MOBA_HEADS_PER_STEP = 2
POOL_WINDOWS = (2, 4, 8, 16)
POOL_HALO = 16
MOE_TOPK = 2
LN_EPS = 1e-5
LOG2_E = float(np.log2(np.e))
LANES = 128
VMEM_LIMIT = 56 * 1024 * 1024

EXPERT_CHUNK = 1024
EXPERT_SUB = 512
GATHER_ROWS = 256


def _tile(n, pref, mult=LANES):
    if n <= pref:
        return n
    t = (pref // mult) * mult
    while t > mult and n % t:
        t -= mult
    assert n % t == 0, (n, pref)
    return t


def _params(*sem):
    return pltpu.CompilerParams(dimension_semantics=sem, vmem_limit_bytes=VMEM_LIMIT)


def _layer_norm(v, g, b):
    mu = jnp.mean(v, axis=-1, keepdims=True)
    d = v - mu
    var = jnp.mean(d * d, axis=-1, keepdims=True)
    return d * lax.rsqrt(var + LN_EPS) * g + b


def _sigmoid(v):
    return 1.0 / (1.0 + jnp.exp(-v))


def _dot(a, b):
    return jnp.dot(a, b, preferred_element_type=F32)


def _dot_nt(a, b):
    return lax.dot_general(a, b, (((1,), (1,)), ((), ())), preferred_element_type=F32)


def _ln_kernel(x_ref, g_ref, b_ref, xf_ref, xb_ref):
    y = _layer_norm(x_ref[...], g_ref[...], b_ref[...])
    xf_ref[...] = y
    xb_ref[...] = y.astype(BF16)


def _ln_in(x, g, b):
    S, D = x.shape
    tm = _tile(S, 512, 8)
    row = pl.BlockSpec((tm, D), lambda i: (i, 0))
    vec = pl.BlockSpec((1, D), lambda i: (0, 0))
    return pl.pallas_call(
        _ln_kernel, name="ln_in",
        grid=(S // tm,),
        in_specs=[row, vec, vec],
        out_specs=[row, row],
        out_shape=[jax.ShapeDtypeStruct((S, D), F32), jax.ShapeDtypeStruct((S, D), BF16)],
        compiler_params=_params("parallel"),
    )(x, g.reshape(1, D), b.reshape(1, D))


def _add_ln_kernel(x_ref, f_ref, g_ref, b_ref, xf_ref, xb_ref, *, alpha):
    y = _layer_norm(alpha * x_ref[...] + f_ref[...], g_ref[...], b_ref[...])
    xf_ref[...] = y
    xb_ref[...] = y.astype(BF16)


def _add_ln(x, f, g, b, alpha):
    S, D = x.shape
    tm = _tile(S, 512, 8)
    row = pl.BlockSpec((tm, D), lambda i: (i, 0))
    vec = pl.BlockSpec((1, D), lambda i: (0, 0))
    return pl.pallas_call(
        functools.partial(_add_ln_kernel, alpha=alpha), name="add_ln",
        grid=(S // tm,),
        in_specs=[row, row, vec, vec],
        out_specs=[row, row],
        out_shape=[jax.ShapeDtypeStruct((S, D), F32), jax.ShapeDtypeStruct((S, D), BF16)],
        compiler_params=_params("parallel"),
    )(x, f, g.reshape(1, D), b.reshape(1, D))


def _proj_kernel(a_ref, b_ref, o_ref, *, mode, q_tiles, q_scale):
    acc = _dot(a_ref[...], b_ref[0])
    if mode == "qkv":
        acc = acc * jnp.where(pl.program_id(1) < q_tiles, q_scale, 1.0).astype(F32)
    elif mode == "sigmoid":
        acc = _sigmoid(acc)
    o_ref[...] = acc.astype(o_ref.dtype)


def _proj(xb, w_in, layer, col0, width, out_dtype, mode, q_width=0):
    S, D = xb.shape
    tm = _tile(S, 1024, 8)
    tn = _tile(int(np.gcd.reduce([width, col0 or width, q_width or width])), 1024)
    assert col0 % tn == 0 and width % tn == 0 and q_width % tn == 0
    off = col0 // tn
    kern = functools.partial(_proj_kernel, mode=mode, q_tiles=q_width // tn,
                             q_scale=HEAD_DIM ** -0.5 * LOG2_E)
    return pl.pallas_call(
        kern, name="proj_" + mode,
        grid=(S // tm, width // tn),
        in_specs=[pl.BlockSpec((tm, D), lambda i, n: (i, 0)),
                  pl.BlockSpec((1, D, tn), lambda i, n: (layer, 0, off + n))],
        out_specs=pl.BlockSpec((tm, tn), lambda i, n: (i, n)),
        out_shape=jax.ShapeDtypeStruct((S, width), out_dtype),
        compiler_params=_params("parallel", "arbitrary"),
    )(xb, w_in)


def _pool_kernel(u_ref, uprev_ref, pw_ref, sc_ref, o_ref, ext_ref, *, tp, group):
    i = pl.program_id(0)
    halo = uprev_ref[...]
    ext_ref[0:POOL_HALO, :] = jnp.where(i > 0, halo, jnp.zeros_like(halo))
    ext_ref[POOL_HALO:, :] = u_ref[...]
    t = i * tp + lax.broadcasted_iota(jnp.int32, (tp, 1), 0)
    for g, w in enumerate(POOL_WINDOWS):
        cols = slice(g * group, (g + 1) * group)
        s = ext_ref[:, cols]
        k = 1
        while k < w:
            s = s + pltpu.roll(s, k, 0)
            k *= 2
        cnt = jnp.minimum(t + 1, w).astype(F32)
        cur = u_ref[:, cols]
        pooled = s[POOL_HALO:, :] / cnt - cur
        y = _dot(pooled.astype(BF16), pw_ref[g]) * sc_ref[:, cols]
        o_ref[:, cols] = y.astype(o_ref.dtype)


def _pool(u, pool_w_b, pool_scale):
    S, PW = u.shape
    G, C, _ = pool_w_b.shape
    assert G == len(POOL_WINDOWS) and G * C == PW
    tp = _tile(S, 512, POOL_HALO)
    per = tp // POOL_HALO
    return pl.pallas_call(
        functools.partial(_pool_kernel, tp=tp, group=C), name="pool",
        grid=(S // tp,),
        in_specs=[pl.BlockSpec((tp, PW), lambda i: (i, 0)),
                  pl.BlockSpec((POOL_HALO, PW), lambda i: (jnp.maximum(i * per - 1, 0), 0)),
                  pl.BlockSpec((G, C, C), lambda i: (0, 0, 0)),
                  pl.BlockSpec((1, PW), lambda i: (0, 0))],
        out_specs=pl.BlockSpec((tp, PW), lambda i: (i, 0)),
        out_shape=jax.ShapeDtypeStruct((S, PW), BF16),
        scratch_shapes=[pltpu.VMEM((tp + POOL_HALO, PW), F32)],
        compiler_params=_params("parallel"),
    )(u, u, pool_w_b, pool_scale.reshape(1, PW))


def _moba_kernel(q_ref, k_ref, v_ref, o_ref, kmean_ref, vt_ref, bias_ref, sa_ref, sb_ref,
                 *, nb, group, heads):
    i = pl.program_id(1)
    B = MOBA_BLOCK
    neg = jnp.float32(-jnp.inf)
    U = group
    last = nb // U - 1
    lanes = [slice(h * HEAD_DIM, (h + 1) * HEAD_DIM) for h in range(heads)]

    @pl.when(i == 0)
    def _():
        def prep(g, carry):
            for h in range(heads):
                for u in range(U):
                    rj = pl.multiple_of((g * U + u) * B, B)
                    vt_ref[h, g, :, u * B:(u + 1) * B] = (
                        v_ref[pl.ds(rj, B), lanes[h]].astype(F32).T.astype(BF16))
                    kj = k_ref[pl.ds(rj, B), lanes[h]].astype(F32)
                    kmean_ref[h, pl.ds(g * U + u, 1), :] = jnp.sum(kj, axis=0, keepdims=True) / B
            return carry

        lax.fori_loop(0, nb // U, prep, 0)

    row0 = pl.multiple_of(i * B, B)
    blk = lax.broadcasted_iota(jnp.int32, (nb, B), 0)
    key_id = lax.broadcasted_iota(jnp.int32, (B, B), 0)
    qry_id = lax.broadcasted_iota(jnp.int32, (B, B), 1)
    qts, init = [], []
    for h in range(heads):
        qt = q_ref[:, lanes[h]].astype(F32).T.astype(BF16)
        km = kmean_ref[h]
        km_hi = km.astype(BF16)
        km_lo = (km - km_hi.astype(F32)).astype(BF16)
        gate = _dot(km_hi, qt) + _dot(km_lo, qt)

        g = jnp.where(blk < i, gate, neg)
        bias = jnp.full((nb, B), neg, F32)
        for _ in range(MOBA_TOPK):
            m = jnp.max(g, axis=0, keepdims=True)
            idx = jnp.min(jnp.where(g == m, blk, nb), axis=0, keepdims=True)
            hit = blk == idx
            bias = jnp.where(hit & (m > neg), 0.0, bias)
            g = jnp.where(hit, neg, g)
        bias_ref[h, 0:nb, :] = bias
        bias_ref[h, nb:, :] = jnp.full((U, B), neg, F32)

        s = _dot(k_ref[pl.ds(row0, B), lanes[h]], qt)
        s = jnp.where(key_id <= qry_id, s, neg)
        m0 = jnp.max(s, axis=0, keepdims=True)
        p = jnp.exp2(s - m0)
        l0 = jnp.sum(p, axis=0, keepdims=True)
        vt_own = v_ref[pl.ds(row0, B), lanes[h]].astype(F32).T.astype(BF16)
        acc0 = _dot(vt_own, p.astype(BF16))
        qts.append(qt)
        init.append((m0, l0, acc0))

    def scores(h, g):
        r0 = pl.multiple_of(jnp.minimum(g, last) * (U * B), U * B)
        return _dot(k_ref[pl.ds(r0, U * B), lanes[h]], qts[h])

    def process(h, s_ref, g, state):
        m_run, l_run, acc = state
        sel_u = [bias_ref[h, pl.ds(g * U + u, 1), :] for u in range(U)]
        s_u = [s_ref[h, u * B:(u + 1) * B, :] for u in range(U)]
        m_new = m_run
        for u in range(U):
            m_new = jnp.maximum(m_new, jnp.max(s_u[u], axis=0, keepdims=True) + sel_u[u])
        a = jnp.exp2(m_run - m_new)
        p_u = [jnp.exp2(s_u[u] - (m_new - sel_u[u])) for u in range(U)]
        l_new = a * l_run
        for u in range(U):
            l_new = l_new + jnp.sum(p_u[u], axis=0, keepdims=True)
        p_all = jnp.concatenate([pu.astype(BF16) for pu in p_u], axis=0)
        acc_new = a * acc + _dot(vt_ref[h, jnp.minimum(g, last)], p_all)
        return m_new, l_new, acc_new

    def pair(t, states):
        for h in range(heads):
            sb_ref[h] = scores(h, 2 * t + 1)
        states = tuple(process(h, sa_ref, 2 * t, states[h]) for h in range(heads))
        for h in range(heads):
            sa_ref[h] = scores(h, 2 * t + 2)
        return tuple(process(h, sb_ref, 2 * t + 1, states[h]) for h in range(heads))

    n_groups = (i + U - 1) // U
    for h in range(heads):
        sa_ref[h] = scores(h, 0)
    final = lax.fori_loop(0, (n_groups + 1) // 2, pair, tuple(init))
    for h in range(heads):
        _, l_fin, acc_fin = final[h]
        o_ref[:, lanes[h]] = (acc_fin / l_fin).T.astype(o_ref.dtype)


def _moba(qkv, n_heads):
    S = qkv.shape[0]
    assert S % MOBA_BLOCK == 0
    nb = S // MOBA_BLOCK
    H = n_heads
    group = MOBA_GROUP
    hp = MOBA_HEADS_PER_STEP
    assert nb % group == 0 and H % hp == 0
    G = H // hp
    W = hp * HEAD_DIM
    return pl.pallas_call(
        functools.partial(_moba_kernel, nb=nb, group=group, heads=hp), name="moba",
        grid=(G, nb),
        in_specs=[pl.BlockSpec((MOBA_BLOCK, W), lambda h, i: (i, h)),
                  pl.BlockSpec((S, W), lambda h, i: (0, G + h)),
                  pl.BlockSpec((S, W), lambda h, i: (0, 2 * G + h))],
        out_specs=pl.BlockSpec((MOBA_BLOCK, W), lambda h, i: (i, h)),
        out_shape=jax.ShapeDtypeStruct((S, H * HEAD_DIM), BF16),
        scratch_shapes=[pltpu.VMEM((hp, nb, HEAD_DIM), F32),
                        pltpu.VMEM((hp, nb // group, HEAD_DIM, group * MOBA_BLOCK), BF16),
                        pltpu.VMEM((hp, nb + group, MOBA_BLOCK), F32),
                        pltpu.VMEM((hp, group * MOBA_BLOCK, MOBA_BLOCK), F32),
                        pltpu.VMEM((hp, group * MOBA_BLOCK, MOBA_BLOCK), F32)],
        compiler_params=_params("parallel", "arbitrary"),
    )(qkv, qkv, qkv)


def _mix_up_kernel(p_ref, a_ref, wp_ref, wa_ref, gp_ref, ga_ref, o_ref):
    y_pool = _dot(p_ref[...], wp_ref[...])
    y_attn = _dot(a_ref[...], wa_ref[...])
    o_ref[...] = (gp_ref[...] * y_pool + ga_ref[...] * y_attn).astype(o_ref.dtype)


def _mix_up(p, o, w_up_pool_b, w_up_attn_b, gates):
    S, PW = p.shape
    AW = o.shape[1]
    D = w_up_pool_b.shape[1]
    tm = _tile(S, 512, 8)
    tn = _tile(D, 1024)
    nt = D // tn
    return pl.pallas_call(
        _mix_up_kernel, name="mix_up",
        grid=(S // tm, nt),
        in_specs=[pl.BlockSpec((tm, PW), lambda i, n: (i, 0)),
                  pl.BlockSpec((tm, AW), lambda i, n: (i, 0)),
                  pl.BlockSpec((PW, tn), lambda i, n: (0, n)),
                  pl.BlockSpec((AW, tn), lambda i, n: (0, n)),
                  pl.BlockSpec((tm, tn), lambda i, n: (i, n)),
                  pl.BlockSpec((tm, tn), lambda i, n: (i, nt + n))],
        out_specs=pl.BlockSpec((tm, tn), lambda i, n: (i, n)),
        out_shape=jax.ShapeDtypeStruct((S, D), BF16),
        compiler_params=_params("parallel", "arbitrary"),
    )(p, o, w_up_pool_b, w_up_attn_b, gates, gates)


def _mix_out_kernel(m_ref, w_ref, x_ref, g_ref, b_ref, xf_ref, xb_ref, *, alpha):
    mix = _dot(m_ref[...], w_ref[...])
    y = _layer_norm(alpha * x_ref[...] + mix, g_ref[...], b_ref[...])
    xf_ref[...] = y
    xb_ref[...] = y.astype(BF16)


def _mix_out(m, w_o_b, x, g, b, alpha):
    S, D = x.shape
    tm = _tile(S, 256, 8)
    row = lambda dt: pl.BlockSpec((tm, D), lambda i: (i, 0))
    vec = pl.BlockSpec((1, D), lambda i: (0, 0))
    return pl.pallas_call(
        functools.partial(_mix_out_kernel, alpha=alpha), name="mix_out",
        grid=(S // tm,),
        in_specs=[row(BF16), pl.BlockSpec((D, D), lambda i: (0, 0)), row(F32), vec, vec],
        out_specs=[row(F32), row(BF16)],
        out_shape=[jax.ShapeDtypeStruct((S, D), F32), jax.ShapeDtypeStruct((S, D), BF16)],
        compiler_params=_params("parallel"),
    )(m, w_o_b, x, g.reshape(1, D), b.reshape(1, D))


def _swiglu_kernel(nvalid_ref, eid_ref, x_ref, wg_ref, wu_ref, wd_ref, o_ref, *, sub):
    c = pl.program_id(0)
    f = pl.program_id(1)
    nv = nvalid_ref[c]
    rows = x_ref.shape[0]
    @pl.when(f == 0)
    def _():
        o_ref[...] = jnp.zeros(o_ref.shape, F32)

    for s0 in range(0, rows, sub):
        rs = slice(s0, s0 + sub)

        @pl.when(nv > s0)
        def _():
            xs = x_ref[rs, :]
            gate = _dot(xs, wg_ref[0].astype(BF16))
            up = _dot(xs, wu_ref[0].astype(BF16))
            h = (gate * _sigmoid(gate) * up).astype(BF16)
            o_ref[rs, :] += _dot(h, wd_ref[0].astype(BF16))


def _swiglu(xrows, w_gate, w_up, w_down, nvalid, eid, w_base, chunk, tf_pref):
    R, D = xrows.shape
    F = w_gate.shape[-1]
    tf = _tile(F, tf_pref)
    sub = min(EXPERT_SUB, chunk)
    assert R % chunk == 0 and chunk % sub == 0

    def w_in_map(c, f, nv, e):
        return (w_base + e[c], 0, jnp.where(nv[c] > 0, f, 0))

    def w_out_map(c, f, nv, e):
        return (w_base + e[c], jnp.where(nv[c] > 0, f, 0), 0)

    grid_spec = pltpu.PrefetchScalarGridSpec(
        num_scalar_prefetch=2,
        grid=(R // chunk, F // tf),
        in_specs=[pl.BlockSpec((chunk, D), lambda c, f, nv, e: (c, 0)),
                  pl.BlockSpec((1, D, tf), w_in_map),
                  pl.BlockSpec((1, D, tf), w_in_map),
                  pl.BlockSpec((1, tf, D), w_out_map)],
        out_specs=pl.BlockSpec((chunk, D), lambda c, f, nv, e: (c, 0)),
    )
    return pl.pallas_call(
        functools.partial(_swiglu_kernel, sub=sub), name="swiglu",
        grid_spec=grid_spec,
        out_shape=jax.ShapeDtypeStruct((R, D), F32),
        compiler_params=_params("parallel", "arbitrary"),
    )(nvalid, eid, xrows, w_gate, w_up, w_down)


def _router_kernel(x_ref, rh_ref, rl_ref, o_ref):
    x = x_ref[...]
    xh = x.astype(BF16)
    xl = (x - xh.astype(F32)).astype(BF16)
    o_ref[...] = _dot(xh, rh_ref[...]) + _dot(xl, rh_ref[...]) + _dot(xh, rl_ref[...])


def _router(x, router_w):
    S, D = x.shape
    E = router_w.shape[1]
    wpad = jnp.pad(router_w, ((0, 0), (0, LANES - E)))
    rh = wpad.astype(BF16)
    rl = (wpad - rh.astype(F32)).astype(BF16)
    tm = _tile(S, 512, 8)
    wspec = pl.BlockSpec((D, LANES), lambda i: (0, 0))
    logits = pl.pallas_call(
        _router_kernel, name="router",
        grid=(S // tm,),
        in_specs=[pl.BlockSpec((tm, D), lambda i: (i, 0)), wspec, wspec],
        out_specs=pl.BlockSpec((tm, LANES), lambda i: (i, 0)),
        out_shape=jax.ShapeDtypeStruct((S, LANES), F32),
        compiler_params=_params("parallel"),
    )(x, rh, rl)
    return logits[:, :E]


def _gather_kernel(tok_ref, valid_ref, x_hbm, o_ref, buf_ref, sem, *, rows):
    b = pl.program_id(0)
    base = b * rows

    @pl.when(valid_ref[b] > 0)
    def _():
        def issue(r, carry):
            tok = tok_ref[base + r]
            pltpu.make_async_copy(x_hbm.at[pl.ds(tok, 1), :], buf_ref.at[pl.ds(r, 1), :], sem).start()
            return carry

        lax.fori_loop(0, rows, issue, 0)
        pltpu.make_async_copy(x_hbm.at[pl.ds(0, rows), :], buf_ref, sem).wait()
        o_ref[...] = buf_ref[...].astype(o_ref.dtype)

    @pl.when(valid_ref[b] == 0)
    def _():
        o_ref[...] = jnp.zeros(o_ref.shape, o_ref.dtype)


def _gather_rows(x, row_tok, blk_valid, n_rows):
    S, D = x.shape
    rows = GATHER_ROWS
    grid_spec = pltpu.PrefetchScalarGridSpec(
        num_scalar_prefetch=2,
        grid=(n_rows // rows,),
        in_specs=[pl.BlockSpec(memory_space=pl.ANY)],
        out_specs=pl.BlockSpec((rows, D), lambda b, t, v: (b, 0)),
        scratch_shapes=[pltpu.VMEM((rows, D), F32), pltpu.SemaphoreType.DMA(())],
    )
    return pl.pallas_call(
        functools.partial(_gather_kernel, rows=rows), name="gather_rows",
        grid_spec=grid_spec,
        out_shape=jax.ShapeDtypeStruct((n_rows, D), BF16),
        compiler_params=_params("arbitrary"),
    )(row_tok, blk_valid, x)


def _combine_kernel(pos_ref, y_hbm, w_ref, x_ref, g_ref, b_ref, xf_ref, xb_ref, buf_ref, sem,
                    *, rows, alpha):
    base = pl.program_id(0) * rows * MOE_TOPK

    def issue(r, carry):
        for k in range(MOE_TOPK):
            src = pos_ref[base + r * MOE_TOPK + k]
            pltpu.make_async_copy(y_hbm.at[pl.ds(src, 1), :], buf_ref.at[k, pl.ds(r, 1), :], sem).start()
        return carry

    lax.fori_loop(0, rows, issue, 0)
    for k in range(MOE_TOPK):
        pltpu.make_async_copy(y_hbm.at[pl.ds(0, rows), :], buf_ref.at[k], sem).wait()
    w = w_ref[...]
    f = w[:, 0:1] * buf_ref[0] + w[:, 1:2] * buf_ref[1]
    y = _layer_norm(alpha * x_ref[...] + f, g_ref[...], b_ref[...])
    xf_ref[...] = y
    xb_ref[...] = y.astype(BF16)


def _combine_ln(y, pos, top_w, x, g, b, alpha):
    S, D = x.shape
    rows = _tile(S, GATHER_ROWS, 8)
    row = pl.BlockSpec((rows, D), lambda i, p: (i, 0))
    vec = pl.BlockSpec((1, D), lambda i, p: (0, 0))
    grid_spec = pltpu.PrefetchScalarGridSpec(
        num_scalar_prefetch=1,
        grid=(S // rows,),
        in_specs=[pl.BlockSpec(memory_space=pl.ANY),
                  pl.BlockSpec((rows, MOE_TOPK), lambda i, p: (i, 0)),
                  row, vec, vec],
        out_specs=[row, row],
        scratch_shapes=[pltpu.VMEM((MOE_TOPK, rows, D), F32), pltpu.SemaphoreType.DMA(())],
    )
    return pl.pallas_call(
        functools.partial(_combine_kernel, rows=rows, alpha=alpha), name="combine_ln",
        grid_spec=grid_spec,
        out_shape=[jax.ShapeDtypeStruct((S, D), F32), jax.ShapeDtypeStruct((S, D), BF16)],
        compiler_params=_params("arbitrary"),
    )(pos, y, top_w, x, g.reshape(1, D), b.reshape(1, D))


def _route(logits, chunk):
    S, E = logits.shape
    top_logit, top_e = lax.top_k(logits, MOE_TOPK)
    top_w = jax.nn.softmax(top_logit, axis=-1)
    n_assign = S * MOE_TOPK
    flat_e = top_e.reshape(-1).astype(jnp.int32)
    onehot = (flat_e[:, None] == jnp.arange(E, dtype=jnp.int32)[None, :]).astype(jnp.int32)
    running = jnp.cumsum(onehot, axis=0)
    rank = jnp.sum(running * onehot, axis=1) - 1
    counts = running[-1]
    padded = (counts + chunk - 1) // chunk * chunk
    ends = jnp.cumsum(padded)
    pstart = ends - padded
    dest = pstart[flat_e] + rank
    n_rows = (-(-n_assign // chunk) + E) * chunk
    n_chunks = n_rows // chunk
    flat_tok = jnp.arange(n_assign, dtype=jnp.int32) // MOE_TOPK
    row_tok = jnp.zeros((n_rows,), jnp.int32).at[dest].set(flat_tok)
    chunk_start = jnp.arange(n_chunks, dtype=jnp.int32) * chunk
    eid = jnp.minimum(jnp.searchsorted(ends, chunk_start, side="right"), E - 1).astype(jnp.int32)
    nvalid = jnp.clip(counts[eid] - (chunk_start - pstart[eid]), 0, chunk).astype(jnp.int32)
    per = chunk // GATHER_ROWS
    blk = jnp.arange(n_rows // GATHER_ROWS, dtype=jnp.int32)
    blk_valid = (nvalid[blk // per] > (blk % per) * GATHER_ROWS).astype(jnp.int32)
    return top_w, dest.astype(jnp.int32), row_tok, eid, nvalid, blk_valid, n_rows


def kernel(x, ln_in_g, ln_in_b, w_in, pool_w, pool_scale, w_up_pool, w_up_attn, w_o, ln_mix_g, ln_mix_b, ffn_w_gate, ffn_w_up, ffn_w_down, moe_router, moe_w_gate, moe_w_up, moe_w_down, ln_ffn_g, ln_ffn_b):
    B, S, D = x.shape
    assert B == 1
    depth = w_in.shape[0]
    PW = w_up_pool.shape[1]
    AW = w_up_attn.shape[1]
    n_heads = AW // HEAD_DIM
    n_exp = moe_router.shape[-1]
    alpha = float((2 * depth) ** 0.25)

    w_in_b = w_in.astype(BF16)
    pool_w_b = pool_w.astype(BF16)
    w_up_pool_b = w_up_pool.astype(BF16)
    w_up_attn_b = w_up_attn.astype(BF16)
    w_o_b = w_o.astype(BF16)
    ffn_gate_b = ffn_w_gate.astype(BF16)
    ffn_up_b = ffn_w_up.astype(BF16)
    ffn_down_b = ffn_w_down.astype(BF16)
    moe_gate = moe_w_gate.reshape((-1,) + moe_w_gate.shape[2:])
    moe_up = moe_w_up.reshape((-1,) + moe_w_up.shape[2:])
    moe_down = moe_w_down.reshape((-1,) + moe_w_down.shape[2:])

    chunk = min(EXPERT_CHUNK, S)
    dense_nvalid = jnp.full((S // chunk,), chunk, jnp.int32)
    dense_eid = jnp.zeros((S // chunk,), jnp.int32)

    xf, xb = _ln_in(x.reshape(S, D), ln_in_g, ln_in_b)
    for l in range(depth):
        u = _proj(xb, w_in_b, l, 0, PW, F32, "plain")
        qkv = _proj(xb, w_in_b, l, PW, 3 * AW, BF16, "qkv", q_width=AW)
        gates = _proj(xb, w_in_b, l, PW + 3 * AW, 2 * D, F32, "sigmoid")
        p = _pool(u, pool_w_b[l], pool_scale[l])
        o = _moba(qkv, n_heads)
        m = _mix_up(p, o, w_up_pool_b[l], w_up_attn_b[l], gates)
        xf, xb = _mix_out(m, w_o_b[l], xf, ln_mix_g[l], ln_mix_b[l], alpha)
        i = l // 2
        if l % 2 == 0:
            f = _swiglu(xb, ffn_gate_b, ffn_up_b, ffn_down_b, dense_nvalid, dense_eid, i, chunk, 512)
            xf, xb = _add_ln(xf, f, ln_ffn_g[l], ln_ffn_b[l], alpha)
        else:
            logits = _router(xf, moe_router[i])
            top_w, dest, row_tok, eid, nvalid, blk_valid, n_rows = _route(logits, chunk)
            xg = _gather_rows(xf, row_tok, blk_valid, n_rows)
            y = _swiglu(xg, moe_gate, moe_up, moe_down, nvalid, eid, i * n_exp, chunk, 512)
            xf, xb = _combine_ln(y, dest, top_w, xf, ln_ffn_g[l], ln_ffn_b[l], alpha)
    return xf.reshape(B, S, D)
```

```python
import functools

import jax
import jax.numpy as jnp
import numpy as np
from jax import lax
from jax.experimental import pallas as pl
from jax.experimental.pallas import tpu as pltpu

F32 = jnp.float32
BF16 = jnp.bfloat16

HEAD_DIM = 128
MOBA_BLOCK = 256
MOBA_TOPK = 3
MOBA_GROUP = 4
MOBA_HEADS_PER_STEP = 2
POOL_WINDOWS = (2, 4, 8, 16)
POOL_HALO = 16
MOE_TOPK = 2
LN_EPS = 1e-5
LOG2_E = float(np.log2(np.e))
LANES = 128
VMEM_LIMIT = 56 * 1024 * 1024

EXPERT_CHUNK = 1024
EXPERT_SUB = 512
GATHER_ROWS = 256
DMA_ISSUE_UNROLL = 8


def _tile(n, pref, mult=LANES):
    if n <= pref:
        return n
    t = (pref // mult) * mult
    while t > mult and n % t:
        t -= mult
    assert n % t == 0, (n, pref)
    return t


def _params(*sem):
    return pltpu.CompilerParams(dimension_semantics=sem, vmem_limit_bytes=VMEM_LIMIT)


def _layer_norm(v, g, b):
    mu = jnp.mean(v, axis=-1, keepdims=True)
    d = v - mu
    var = jnp.mean(d * d, axis=-1, keepdims=True)
    return d * lax.rsqrt(var + LN_EPS) * g + b


def _sigmoid(v):
    return 1.0 / (1.0 + jnp.exp(-v))


def _dot(a, b):
    return jnp.dot(a, b, preferred_element_type=F32)


def _dot_nt(a, b):
    return lax.dot_general(a, b, (((1,), (1,)), ((), ())), preferred_element_type=F32)


def _ln_kernel(x_ref, g_ref, b_ref, xf_ref, xb_ref):
    y = _layer_norm(x_ref[...], g_ref[...], b_ref[...])
    xf_ref[...] = y
    xb_ref[...] = y.astype(BF16)


def _ln_in(x, g, b):
    S, D = x.shape
    tm = _tile(S, 512, 8)
    row = pl.BlockSpec((tm, D), lambda i: (i, 0))
    vec = pl.BlockSpec((1, D), lambda i: (0, 0))
    return pl.pallas_call(
        _ln_kernel, name="ln_in",
        grid=(S // tm,),
        in_specs=[row, vec, vec],
        out_specs=[row, row],
        out_shape=[jax.ShapeDtypeStruct((S, D), F32), jax.ShapeDtypeStruct((S, D), BF16)],
        compiler_params=_params("parallel"),
    )(x, g.reshape(1, D), b.reshape(1, D))


def _add_ln_kernel(x_ref, f_ref, g_ref, b_ref, xf_ref, xb_ref, *, alpha):
    y = _layer_norm(alpha * x_ref[...] + f_ref[...], g_ref[...], b_ref[...])
    xf_ref[...] = y
    xb_ref[...] = y.astype(BF16)


def _add_ln(x, f, g, b, alpha):
    S, D = x.shape
    tm = _tile(S, 512, 8)
    row = pl.BlockSpec((tm, D), lambda i: (i, 0))
    vec = pl.BlockSpec((1, D), lambda i: (0, 0))
    return pl.pallas_call(
        functools.partial(_add_ln_kernel, alpha=alpha), name="add_ln",
        grid=(S // tm,),
        in_specs=[row, row, vec, vec],
        out_specs=[row, row],
        out_shape=[jax.ShapeDtypeStruct((S, D), F32), jax.ShapeDtypeStruct((S, D), BF16)],
        compiler_params=_params("parallel"),
    )(x, f, g.reshape(1, D), b.reshape(1, D))


def _proj_kernel(a_ref, b_ref, o_ref, bb_ref, *, mode, q_tiles, q_scale):
    @pl.when(pl.program_id(1) == 0)
    def _():
        bb_ref[...] = b_ref[0].astype(BF16)

    acc = _dot(a_ref[...], bb_ref[...])
    if mode == "qkv":
        acc = acc * jnp.where(pl.program_id(0) < q_tiles, q_scale, 1.0).astype(F32)
    elif mode == "sigmoid":
        acc = _sigmoid(acc)
    o_ref[...] = acc.astype(o_ref.dtype)


def _proj(xb, w_in, layer, col0, width, out_dtype, mode, q_width=0):
    S, D = xb.shape
    tm = _tile(S, 1024, 8)
    tn = _tile(int(np.gcd.reduce([width, col0 or width, q_width or width])), 1024)
    assert col0 % tn == 0 and width % tn == 0 and q_width % tn == 0
    off = col0 // tn
    kern = functools.partial(_proj_kernel, mode=mode, q_tiles=q_width // tn,
                             q_scale=HEAD_DIM ** -0.5 * LOG2_E)
    return pl.pallas_call(
        kern, name="proj_" + mode,
        grid=(width // tn, S // tm),
        in_specs=[pl.BlockSpec((tm, D), lambda n, i: (i, 0)),
                  pl.BlockSpec((1, D, tn), lambda n, i: (layer, 0, off + n))],
        out_specs=pl.BlockSpec((tm, tn), lambda n, i: (i, n)),
        out_shape=jax.ShapeDtypeStruct((S, width), out_dtype),
        scratch_shapes=[pltpu.VMEM((D, tn), BF16)],
        compiler_params=_params("parallel", "arbitrary"),
    )(xb, w_in)


def _pool_kernel(u_ref, uprev_ref, pw_ref, sc_ref, o_ref, ext_ref, *, tp, group):
    i = pl.program_id(0)
    halo = uprev_ref[...]
    ext_ref[0:POOL_HALO, :] = jnp.where(i > 0, halo, jnp.zeros_like(halo))
    ext_ref[POOL_HALO:, :] = u_ref[...]
    t = i * tp + lax.broadcasted_iota(jnp.int32, (tp, 1), 0)
    for g, w in enumerate(POOL_WINDOWS):
        cols = slice(g * group, (g + 1) * group)
        s = ext_ref[:, cols]
        k = 1
        while k < w:
            s = s + pltpu.roll(s, k, 0)
            k *= 2
        cnt = jnp.minimum(t + 1, w).astype(F32)
        cur = u_ref[:, cols]
        pooled = s[POOL_HALO:, :] / cnt - cur
        y = _dot(pooled.astype(BF16), pw_ref[g]) * sc_ref[:, cols]
        o_ref[:, cols] = y.astype(o_ref.dtype)


def _pool(u, pool_w_b, pool_scale):
    S, PW = u.shape
    G, C, _ = pool_w_b.shape
    assert G == len(POOL_WINDOWS) and G * C == PW
    tp = _tile(S, 512, POOL_HALO)
    per = tp // POOL_HALO
    return pl.pallas_call(
        functools.partial(_pool_kernel, tp=tp, group=C), name="pool",
        grid=(S // tp,),
        in_specs=[pl.BlockSpec((tp, PW), lambda i: (i, 0)),
                  pl.BlockSpec((POOL_HALO, PW), lambda i: (jnp.maximum(i * per - 1, 0), 0)),
                  pl.BlockSpec((G, C, C), lambda i: (0, 0, 0)),
                  pl.BlockSpec((1, PW), lambda i: (0, 0))],
        out_specs=pl.BlockSpec((tp, PW), lambda i: (i, 0)),
        out_shape=jax.ShapeDtypeStruct((S, PW), BF16),
        scratch_shapes=[pltpu.VMEM((tp + POOL_HALO, PW), F32)],
        compiler_params=_params("parallel"),
    )(u, u, pool_w_b, pool_scale.reshape(1, PW))


def _moba_kernel(q_ref, k_ref, v_ref, o_ref, kmean_ref, vt_ref, bias_ref, sa_ref, sb_ref,
                 *, nb, group, heads):
    i = pl.program_id(1)
    B = MOBA_BLOCK
    neg = jnp.float32(-jnp.inf)
    U = group
    last = nb // U - 1
    lanes = [slice(h * HEAD_DIM, (h + 1) * HEAD_DIM) for h in range(heads)]

    @pl.when(i == 0)
    def _():
        def prep(g, carry):
            for h in range(heads):
                for u in range(U):
                    rj = pl.multiple_of((g * U + u) * B, B)
                    vt_ref[h, g, :, u * B:(u + 1) * B] = (
                        v_ref[pl.ds(rj, B), lanes[h]].astype(F32).T.astype(BF16))
                    kj = k_ref[pl.ds(rj, B), lanes[h]].astype(F32)
                    kmean_ref[h, pl.ds(g * U + u, 1), :] = jnp.sum(kj, axis=0, keepdims=True) / B
            return carry

        lax.fori_loop(0, nb // U, prep, 0)

    row0 = pl.multiple_of(i * B, B)
    blk = lax.broadcasted_iota(jnp.int32, (nb, B), 0)
    key_id = lax.broadcasted_iota(jnp.int32, (B, B), 0)
    qry_id = lax.broadcasted_iota(jnp.int32, (B, B), 1)
    qts, init = [], []
    for h in range(heads):
        qt = q_ref[:, lanes[h]].astype(F32).T.astype(BF16)
        km = kmean_ref[h]
        km_hi = km.astype(BF16)
        km_lo = (km - km_hi.astype(F32)).astype(BF16)
        gate = _dot(km_hi, qt) + _dot(km_lo, qt)

        g = jnp.where(blk < i, gate, neg)
        bias = jnp.full((nb, B), neg, F32)
        for _ in range(MOBA_TOPK):
            m = jnp.max(g, axis=0, keepdims=True)
            idx = jnp.min(jnp.where(g == m, blk, nb), axis=0, keepdims=True)
            hit = blk == idx
            bias = jnp.where(hit & (m > neg), 0.0, bias)
            g = jnp.where(hit, neg, g)
        bias_ref[h, 0:nb, :] = bias
        bias_ref[h, nb:, :] = jnp.full((U, B), neg, F32)

        s = _dot(k_ref[pl.ds(row0, B), lanes[h]], qt)
        s = jnp.where(key_id <= qry_id, s, neg)
        m0 = jnp.max(s, axis=0, keepdims=True)
        p = jnp.exp2(s - m0)
        l0 = jnp.sum(p, axis=0, keepdims=True)
        vt_own = v_ref[pl.ds(row0, B), lanes[h]].astype(F32).T.astype(BF16)
        acc0 = _dot(vt_own, p.astype(BF16))
        qts.append(qt)
        init.append((m0, l0, acc0))

    def scores(h, g):
        r0 = pl.multiple_of(jnp.minimum(g, last) * (U * B), U * B)
        return _dot(k_ref[pl.ds(r0, U * B), lanes[h]], qts[h])

    def process(h, s_ref, g, state):
        m_run, l_run, acc = state
        sel_u = [bias_ref[h, pl.ds(g * U + u, 1), :] for u in range(U)]
        s_u = [s_ref[h, u * B:(u + 1) * B, :] for u in range(U)]
        m_new = m_run
        for u in range(U):
            m_new = jnp.maximum(m_new, jnp.max(s_u[u], axis=0, keepdims=True) + sel_u[u])
        a = jnp.exp2(m_run - m_new)
        l_new = a * l_run
        acc_new = a * acc
        gv = jnp.minimum(g, last)
        for u in range(U):
            p = jnp.exp2(s_u[u] - (m_new - sel_u[u]))
            l_new = l_new + jnp.sum(p, axis=0, keepdims=True)
            acc_new = acc_new + _dot(vt_ref[h, gv, :, u * B:(u + 1) * B], p.astype(BF16))
        return m_new, l_new, acc_new

    def pair(t, states):
        for h in range(heads):
            sb_ref[h] = scores(h, 2 * t + 1)
        states = tuple(process(h, sa_ref, 2 * t, states[h]) for h in range(heads))
        for h in range(heads):
            sa_ref[h] = scores(h, 2 * t + 2)
        return tuple(process(h, sb_ref, 2 * t + 1, states[h]) for h in range(heads))

    n_groups = (i + U - 1) // U
    for h in range(heads):
        sa_ref[h] = scores(h, 0)
    final = lax.fori_loop(0, (n_groups + 1) // 2, pair, tuple(init))
    for h in range(heads):
        _, l_fin, acc_fin = final[h]
        o_ref[:, lanes[h]] = (acc_fin / l_fin).T.astype(o_ref.dtype)


def _moba(qkv, n_heads):
    S = qkv.shape[0]
    assert S % MOBA_BLOCK == 0
    nb = S // MOBA_BLOCK
    H = n_heads
    group = MOBA_GROUP
    hp = MOBA_HEADS_PER_STEP
    assert nb % group == 0 and H % hp == 0
    G = H // hp
    W = hp * HEAD_DIM
    return pl.pallas_call(
        functools.partial(_moba_kernel, nb=nb, group=group, heads=hp), name="moba",
        grid=(G, nb),
        in_specs=[pl.BlockSpec((MOBA_BLOCK, W), lambda h, i: (i, h)),
                  pl.BlockSpec((S, W), lambda h, i: (0, G + h)),
                  pl.BlockSpec((S, W), lambda h, i: (0, 2 * G + h))],
        out_specs=pl.BlockSpec((MOBA_BLOCK, W), lambda h, i: (i, h)),
        out_shape=jax.ShapeDtypeStruct((S, H * HEAD_DIM), BF16),
        scratch_shapes=[pltpu.VMEM((hp, nb, HEAD_DIM), F32),
                        pltpu.VMEM((hp, nb // group, HEAD_DIM, group * MOBA_BLOCK), BF16),
                        pltpu.VMEM((hp, nb + group, MOBA_BLOCK), F32),
                        pltpu.VMEM((hp, group * MOBA_BLOCK, MOBA_BLOCK), F32),
                        pltpu.VMEM((hp, group * MOBA_BLOCK, MOBA_BLOCK), F32)],
        compiler_params=_params("parallel", "arbitrary"),
    )(qkv, qkv, qkv)


def _mix_up_kernel(p_ref, a_ref, wp_ref, wa_ref, gp_ref, ga_ref, o_ref):
    y_pool = _dot(p_ref[...], wp_ref[...])
    y_attn = _dot(a_ref[...], wa_ref[...])
    o_ref[...] = (gp_ref[...] * y_pool + ga_ref[...] * y_attn).astype(o_ref.dtype)


def _mix_up(p, o, w_up_pool_b, w_up_attn_b, gates):
    S, PW = p.shape
    AW = o.shape[1]
    D = w_up_pool_b.shape[1]
    tm = _tile(S, 512, 8)
    tn = _tile(D, 1024)
    nt = D // tn
    return pl.pallas_call(
        _mix_up_kernel, name="mix_up",
        grid=(S // tm, nt),
        in_specs=[pl.BlockSpec((tm, PW), lambda i, n: (i, 0)),
                  pl.BlockSpec((tm, AW), lambda i, n: (i, 0)),
                  pl.BlockSpec((PW, tn), lambda i, n: (0, n)),
                  pl.BlockSpec((AW, tn), lambda i, n: (0, n)),
                  pl.BlockSpec((tm, tn), lambda i, n: (i, n)),
                  pl.BlockSpec((tm, tn), lambda i, n: (i, nt + n))],
        out_specs=pl.BlockSpec((tm, tn), lambda i, n: (i, n)),
        out_shape=jax.ShapeDtypeStruct((S, D), BF16),
        compiler_params=_params("parallel", "arbitrary"),
    )(p, o, w_up_pool_b, w_up_attn_b, gates, gates)


def _mix_out_kernel(m_ref, w_ref, x_ref, g_ref, b_ref, xf_ref, xb_ref, *, alpha):
    mix = _dot(m_ref[...], w_ref[...])
    y = _layer_norm(alpha * x_ref[...] + mix, g_ref[...], b_ref[...])
    xf_ref[...] = y
    xb_ref[...] = y.astype(BF16)


def _mix_out(m, w_o_b, x, g, b, alpha):
    S, D = x.shape
    tm = _tile(S, 256, 8)
    row = lambda dt: pl.BlockSpec((tm, D), lambda i: (i, 0))
    vec = pl.BlockSpec((1, D), lambda i: (0, 0))
    return pl.pallas_call(
        functools.partial(_mix_out_kernel, alpha=alpha), name="mix_out",
        grid=(S // tm,),
        in_specs=[row(BF16), pl.BlockSpec((D, D), lambda i: (0, 0)), row(F32), vec, vec],
        out_specs=[row(F32), row(BF16)],
        out_shape=[jax.ShapeDtypeStruct((S, D), F32), jax.ShapeDtypeStruct((S, D), BF16)],
        compiler_params=_params("parallel"),
    )(m, w_o_b, x, g.reshape(1, D), b.reshape(1, D))


def _swiglu_kernel(nvalid_ref, eid_ref, x_ref, wg_ref, wu_ref, wd_ref, o_ref, *, sub):
    c = pl.program_id(0)
    f = pl.program_id(1)
    nv = nvalid_ref[c]
    rows = x_ref.shape[0]

    @pl.when(f == 0)
    def _():
        o_ref[...] = jnp.zeros(o_ref.shape, F32)

    def run(n_rows):
        wg = wg_ref[0].astype(BF16)
        wu = wu_ref[0].astype(BF16)
        wd = wd_ref[0].astype(BF16)
        for s0 in range(0, n_rows, sub):
            rs = slice(s0, s0 + sub)
            xs = x_ref[rs, :]
            gate = _dot(xs, wg)
            up = _dot(xs, wu)
            h = (gate * _sigmoid(gate) * up).astype(BF16)
            o_ref[rs, :] += _dot(h, wd)

    n_sub = rows // sub
    for k in range(1, n_sub + 1):
        lo = (k - 1) * sub
        cond = (nv > lo) if k == n_sub else ((nv > lo) & (nv <= lo + sub))
        pl.when(cond)(functools.partial(run, k * sub))


def _swiglu(xrows, w_gate, w_up, w_down, nvalid, eid, w_base, chunk, tf_pref):
    R, D = xrows.shape
    F = w_gate.shape[-1]
    tf = _tile(F, tf_pref)
    sub = min(EXPERT_SUB, chunk)
    assert R % chunk == 0 and chunk % sub == 0

    def w_in_map(c, f, nv, e):
        return (w_base + e[c], 0, jnp.where(nv[c] > 0, f, 0))

    def w_out_map(c, f, nv, e):
        return (w_base + e[c], jnp.where(nv[c] > 0, f, 0), 0)

    grid_spec = pltpu.PrefetchScalarGridSpec(
        num_scalar_prefetch=2,
        grid=(R // chunk, F // tf),
        in_specs=[pl.BlockSpec((chunk, D), lambda c, f, nv, e: (c, 0)),
                  pl.BlockSpec((1, D, tf), w_in_map),
                  pl.BlockSpec((1, D, tf), w_in_map),
                  pl.BlockSpec((1, tf, D), w_out_map)],
        out_specs=pl.BlockSpec((chunk, D), lambda c, f, nv, e: (c, 0)),
    )
    return pl.pallas_call(
        functools.partial(_swiglu_kernel, sub=sub), name="swiglu",
        grid_spec=grid_spec,
        out_shape=jax.ShapeDtypeStruct((R, D), F32),
        compiler_params=_params("parallel", "arbitrary"),
    )(nvalid, eid, xrows, w_gate, w_up, w_down)


def _router_kernel(x_ref, rh_ref, rl_ref, o_ref):
    x = x_ref[...]
    xh = x.astype(BF16)
    xl = (x - xh.astype(F32)).astype(BF16)
    o_ref[...] = _dot(xh, rh_ref[...]) + _dot(xl, rh_ref[...]) + _dot(xh, rl_ref[...])


def _router(x, router_w):
    S, D = x.shape
    E = router_w.shape[1]
    wpad = jnp.pad(router_w, ((0, 0), (0, LANES - E)))
    rh = wpad.astype(BF16)
    rl = (wpad - rh.astype(F32)).astype(BF16)
    tm = _tile(S, 512, 8)
    wspec = pl.BlockSpec((D, LANES), lambda i: (0, 0))
    logits = pl.pallas_call(
        _router_kernel, name="router",
        grid=(S // tm,),
        in_specs=[pl.BlockSpec((tm, D), lambda i: (i, 0)), wspec, wspec],
        out_specs=pl.BlockSpec((tm, LANES), lambda i: (i, 0)),
        out_shape=jax.ShapeDtypeStruct((S, LANES), F32),
        compiler_params=_params("parallel"),
    )(x, rh, rl)
    return logits[:, :E]


def _gather_kernel(tok_ref, valid_ref, x_hbm, o_ref, buf_ref, sem, *, rows, n_blocks):
    b = pl.program_id(0)

    def start(blk, slot):
        base = blk * rows

        def issue(r, carry):
            tok = tok_ref[base + r]
            pltpu.make_async_copy(x_hbm.at[pl.ds(tok, 1), :], buf_ref.at[slot, pl.ds(r, 1), :],
                                  sem.at[slot]).start()
            return carry

        lax.fori_loop(0, rows, issue, 0, unroll=DMA_ISSUE_UNROLL)

    @pl.when((b == 0) & (valid_ref[0] > 0))
    def _():
        start(0, 0)

    nxt = jnp.minimum(b + 1, n_blocks - 1)

    @pl.when((b + 1 < n_blocks) & (valid_ref[nxt] > 0))
    def _():
        start(nxt, nxt % 2)

    @pl.when(valid_ref[b] > 0)
    def _():
        slot = b % 2
        pltpu.make_async_copy(x_hbm.at[pl.ds(0, rows), :], buf_ref.at[slot], sem.at[slot]).wait()
        o_ref[...] = buf_ref[slot].astype(o_ref.dtype)

    @pl.when(valid_ref[b] == 0)
    def _():
        o_ref[...] = jnp.zeros(o_ref.shape, o_ref.dtype)


def _gather_rows(x, row_tok, blk_valid, n_rows):
    S, D = x.shape
    rows = GATHER_ROWS
    grid_spec = pltpu.PrefetchScalarGridSpec(
        num_scalar_prefetch=2,
        grid=(n_rows // rows,),
        in_specs=[pl.BlockSpec(memory_space=pl.ANY)],
        out_specs=pl.BlockSpec((rows, D), lambda b, t, v: (b, 0)),
        scratch_shapes=[pltpu.VMEM((2, rows, D), F32), pltpu.SemaphoreType.DMA((2,))],
    )
    return pl.pallas_call(
        functools.partial(_gather_kernel, rows=rows, n_blocks=n_rows // rows), name="gather_rows",
        grid_spec=grid_spec,
        out_shape=jax.ShapeDtypeStruct((n_rows, D), BF16),
        compiler_params=_params("arbitrary"),
    )(row_tok, blk_valid, x)


def _combine_kernel(pos_ref, y_hbm, w_ref, x_ref, g_ref, b_ref, xf_ref, xb_ref, buf_ref, sem,
                    *, rows, n_blocks, alpha):
    b = pl.program_id(0)

    def start(blk, slot):
        base = blk * rows * MOE_TOPK

        def issue(r, carry):
            for k in range(MOE_TOPK):
                src = pos_ref[base + r * MOE_TOPK + k]
                pltpu.make_async_copy(y_hbm.at[pl.ds(src, 1), :], buf_ref.at[slot, k, pl.ds(r, 1), :],
                                      sem.at[slot]).start()
            return carry

        lax.fori_loop(0, rows, issue, 0, unroll=DMA_ISSUE_UNROLL)

    @pl.when(b == 0)
    def _():
        start(0, 0)

    @pl.when(b + 1 < n_blocks)
    def _():
        start(b + 1, (b + 1) % 2)

    slot = b % 2
    for k in range(MOE_TOPK):
        pltpu.make_async_copy(y_hbm.at[pl.ds(0, rows), :], buf_ref.at[slot, k], sem.at[slot]).wait()
    w = w_ref[...]
    f = w[:, 0:1] * buf_ref[slot, 0] + w[:, 1:2] * buf_ref[slot, 1]
    y = _layer_norm(alpha * x_ref[...] + f, g_ref[...], b_ref[...])
    xf_ref[...] = y
    xb_ref[...] = y.astype(BF16)


def _combine_ln(y, pos, top_w, x, g, b, alpha):
    S, D = x.shape
    rows = _tile(S, GATHER_ROWS, 8)
    row = pl.BlockSpec((rows, D), lambda i, p: (i, 0))
    vec = pl.BlockSpec((1, D), lambda i, p: (0, 0))
    grid_spec = pltpu.PrefetchScalarGridSpec(
        num_scalar_prefetch=1,
        grid=(S // rows,),
        in_specs=[pl.BlockSpec(memory_space=pl.ANY),
                  pl.BlockSpec((rows, MOE_TOPK), lambda i, p: (i, 0)),
                  row, vec, vec],
        out_specs=[row, row],
        scratch_shapes=[pltpu.VMEM((2, MOE_TOPK, rows, D), F32), pltpu.SemaphoreType.DMA((2,))],
    )
    return pl.pallas_call(
        functools.partial(_combine_kernel, rows=rows, n_blocks=S // rows, alpha=alpha), name="combine_ln",
        grid_spec=grid_spec,
        out_shape=[jax.ShapeDtypeStruct((S, D), F32), jax.ShapeDtypeStruct((S, D), BF16)],
        compiler_params=_params("arbitrary"),
    )(pos, y, top_w, x, g.reshape(1, D), b.reshape(1, D))


def _route(logits, chunk):
    S, E = logits.shape
    top_logit, top_e = lax.top_k(logits, MOE_TOPK)
    top_w = jax.nn.softmax(top_logit, axis=-1)
    n_assign = S * MOE_TOPK
    flat_e = top_e.reshape(-1).astype(jnp.int32)
    onehot = (flat_e[:, None] == jnp.arange(E, dtype=jnp.int32)[None, :]).astype(jnp.int32)
    running = jnp.cumsum(onehot, axis=0)
    rank = jnp.sum(running * onehot, axis=1) - 1
    counts = running[-1]
    padded = (counts + chunk - 1) // chunk * chunk
    ends = jnp.cumsum(padded)
    pstart = ends - padded
    dest = pstart[flat_e] + rank
    n_rows = (-(-n_assign // chunk) + E) * chunk
    n_chunks = n_rows // chunk
    flat_tok = jnp.arange(n_assign, dtype=jnp.int32) // MOE_TOPK
    row_tok = jnp.zeros((n_rows,), jnp.int32).at[dest].set(flat_tok)
    chunk_start = jnp.arange(n_chunks, dtype=jnp.int32) * chunk
    eid = jnp.minimum(jnp.searchsorted(ends, chunk_start, side="right"), E - 1).astype(jnp.int32)
    nvalid = jnp.clip(counts[eid] - (chunk_start - pstart[eid]), 0, chunk).astype(jnp.int32)
    per = chunk // GATHER_ROWS
    blk = jnp.arange(n_rows // GATHER_ROWS, dtype=jnp.int32)
    blk_valid = (nvalid[blk // per] > (blk % per) * GATHER_ROWS).astype(jnp.int32)
    return top_w, dest.astype(jnp.int32), row_tok, eid, nvalid, blk_valid, n_rows


def kernel(x, ln_in_g, ln_in_b, w_in, pool_w, pool_scale, w_up_pool, w_up_attn, w_o, ln_mix_g, ln_mix_b, ffn_w_gate, ffn_w_up, ffn_w_down, moe_router, moe_w_gate, moe_w_up, moe_w_down, ln_ffn_g, ln_ffn_b):
    B, S, D = x.shape
    assert B == 1
    depth = w_in.shape[0]
    PW = w_up_pool.shape[1]
    AW = w_up_attn.shape[1]
    n_heads = AW // HEAD_DIM
    n_exp = moe_router.shape[-1]
    alpha = float((2 * depth) ** 0.25)

    pool_w_b = pool_w.astype(BF16)
    w_up_pool_b = w_up_pool.astype(BF16)
    w_up_attn_b = w_up_attn.astype(BF16)
    w_o_b = w_o.astype(BF16)
    moe_gate = moe_w_gate.reshape((-1,) + moe_w_gate.shape[2:])
    moe_up = moe_w_up.reshape((-1,) + moe_w_up.shape[2:])
    moe_down = moe_w_down.reshape((-1,) + moe_w_down.shape[2:])

    chunk = min(EXPERT_CHUNK, S)
    dense_nvalid = jnp.full((S // chunk,), chunk, jnp.int32)
    dense_eid = jnp.zeros((S // chunk,), jnp.int32)

    xf, xb = _ln_in(x.reshape(S, D), ln_in_g, ln_in_b)
    for l in range(depth):
        u = _proj(xb, w_in, l, 0, PW, F32, "plain")
        qkv = _proj(xb, w_in, l, PW, 3 * AW, BF16, "qkv", q_width=AW)
        gates = _proj(xb, w_in, l, PW + 3 * AW, 2 * D, F32, "sigmoid")
        p = _pool(u, pool_w_b[l], pool_scale[l])
        o = _moba(qkv, n_heads)
        m = _mix_up(p, o, w_up_pool_b[l], w_up_attn_b[l], gates)
        xf, xb = _mix_out(m, w_o_b[l], xf, ln_mix_g[l], ln_mix_b[l], alpha)
        i = l // 2
        if l % 2 == 0:
            f = _swiglu(xb, ffn_w_gate, ffn_w_up, ffn_w_down, dense_nvalid, dense_eid, i, chunk, 512)
            xf, xb = _add_ln(xf, f, ln_ffn_g[l], ln_ffn_b[l], alpha)
        else:
            logits = _router(xf, moe_router[i])
            top_w, dest, row_tok, eid, nvalid, blk_valid, n_rows = _route(logits, chunk)
            xg = _gather_rows(xf, row_tok, blk_valid, n_rows)
            y = _swiglu(xg, moe_gate, moe_up, moe_down, nvalid, eid, i * n_exp, chunk, 512)
            xf, xb = _combine_ln(y, dest, top_w, xf, ln_ffn_g[l], ln_ffn_b[l], alpha)
    return xf.reshape(B, S, D)
```

```python
import functools

import jax
import jax.numpy as jnp
import numpy as np
from jax import lax
from jax.experimental import pallas as pl
from jax.experimental.pallas import tpu as pltpu

F32 = jnp.float32
BF16 = jnp.bfloat16

HEAD_DIM = 128
MOBA_BLOCK = 256
MOBA_TOPK = 3
MOBA_GROUP = 2
MOBA_HEADS_PER_STEP = 2
POOL_WINDOWS = (2, 4, 8, 16)
POOL_HALO = 16
MOE_TOPK = 2
LN_EPS = 1e-5
LOG2_E = float(np.log2(np.e))
LANES = 128
VMEM_LIMIT = 56 * 1024 * 1024

EXPERT_CHUNK = 1024
EXPERT_SUB = 512
EXPERT_TAIL = 256
GATHER_ROWS = 256
DMA_ISSUE_UNROLL = 8


def _tile(n, pref, mult=LANES):
    if n <= pref:
        return n
    t = (pref // mult) * mult
    while t > mult and n % t:
        t -= mult
    assert n % t == 0, (n, pref)
    return t


def _params(*sem):
    return pltpu.CompilerParams(dimension_semantics=sem, vmem_limit_bytes=VMEM_LIMIT)


def _layer_norm(v, g, b):
    mu = jnp.mean(v, axis=-1, keepdims=True)
    d = v - mu
    var = jnp.mean(d * d, axis=-1, keepdims=True)
    return d * lax.rsqrt(var + LN_EPS) * g + b


def _sigmoid(v):
    return 1.0 / (1.0 + jnp.exp(-v))


def _dot(a, b):
    return jnp.dot(a, b, preferred_element_type=F32)


def _dot_nt(a, b):
    return lax.dot_general(a, b, (((1,), (1,)), ((), ())), preferred_element_type=F32)


def _ln_kernel(x_ref, g_ref, b_ref, xf_ref, xb_ref):
    y = _layer_norm(x_ref[...], g_ref[...], b_ref[...])
    xf_ref[...] = y
    xb_ref[...] = y.astype(BF16)


def _ln_in(x, g, b):
    S, D = x.shape
    tm = _tile(S, 512, 8)
    row = pl.BlockSpec((tm, D), lambda i: (i, 0))
    vec = pl.BlockSpec((1, D), lambda i: (0, 0))
    return pl.pallas_call(
        _ln_kernel, name="ln_in",
        grid=(S // tm,),
        in_specs=[row, vec, vec],
        out_specs=[row, row],
        out_shape=[jax.ShapeDtypeStruct((S, D), F32), jax.ShapeDtypeStruct((S, D), BF16)],
        compiler_params=_params("parallel"),
    )(x, g.reshape(1, D), b.reshape(1, D))


def _add_ln_kernel(x_ref, f_ref, g_ref, b_ref, xf_ref, xb_ref, *, alpha):
    y = _layer_norm(alpha * x_ref[...] + f_ref[...], g_ref[...], b_ref[...])
    xf_ref[...] = y
    xb_ref[...] = y.astype(BF16)


def _add_ln(x, f, g, b, alpha):
    S, D = x.shape
    tm = _tile(S, 512, 8)
    row = pl.BlockSpec((tm, D), lambda i: (i, 0))
    vec = pl.BlockSpec((1, D), lambda i: (0, 0))
    return pl.pallas_call(
        functools.partial(_add_ln_kernel, alpha=alpha), name="add_ln",
        grid=(S // tm,),
        in_specs=[row, row, vec, vec],
        out_specs=[row, row],
        out_shape=[jax.ShapeDtypeStruct((S, D), F32), jax.ShapeDtypeStruct((S, D), BF16)],
        compiler_params=_params("parallel"),
    )(x, f, g.reshape(1, D), b.reshape(1, D))


def _proj_kernel(a_ref, b_ref, o_ref, bb_ref, *, mode, q_tiles, q_scale):
    @pl.when(pl.program_id(1) == 0)
    def _():
        bb_ref[...] = b_ref[0].astype(BF16)

    acc = _dot(a_ref[...], bb_ref[...])
    if mode == "qkv":
        acc = acc * jnp.where(pl.program_id(0) < q_tiles, q_scale, 1.0).astype(F32)
    elif mode == "sigmoid":
        acc = _sigmoid(acc)
    o_ref[...] = acc.astype(o_ref.dtype)


def _proj(xb, w_in, layer, col0, width, out_dtype, mode, q_width=0):
    S, D = xb.shape
    tm = _tile(S, 1024, 8)
    tn = _tile(int(np.gcd.reduce([width, col0 or width, q_width or width])), 1024)
    assert col0 % tn == 0 and width % tn == 0 and q_width % tn == 0
    off = col0 // tn
    kern = functools.partial(_proj_kernel, mode=mode, q_tiles=q_width // tn,
                             q_scale=HEAD_DIM ** -0.5 * LOG2_E)
    return pl.pallas_call(
        kern, name="proj_" + mode,
        grid=(width // tn, S // tm),
        in_specs=[pl.BlockSpec((tm, D), lambda n, i: (i, 0)),
                  pl.BlockSpec((1, D, tn), lambda n, i: (layer, 0, off + n))],
        out_specs=pl.BlockSpec((tm, tn), lambda n, i: (i, n)),
        out_shape=jax.ShapeDtypeStruct((S, width), out_dtype),
        scratch_shapes=[pltpu.VMEM((D, tn), BF16)],
        compiler_params=_params("parallel", "arbitrary"),
    )(xb, w_in)


def _pool_kernel(u_ref, uprev_ref, pw_ref, sc_ref, o_ref, ext_ref, *, tp, group):
    i = pl.program_id(0)
    halo = uprev_ref[...]
    ext_ref[0:POOL_HALO, :] = jnp.where(i > 0, halo, jnp.zeros_like(halo))
    ext_ref[POOL_HALO:, :] = u_ref[...]
    t = i * tp + lax.broadcasted_iota(jnp.int32, (tp, 1), 0)
    for g, w in enumerate(POOL_WINDOWS):
        cols = slice(g * group, (g + 1) * group)
        s = ext_ref[:, cols]
        k = 1
        while k < w:
            s = s + pltpu.roll(s, k, 0)
            k *= 2
        cnt = jnp.minimum(t + 1, w).astype(F32)
        cur = u_ref[:, cols]
        pooled = s[POOL_HALO:, :] / cnt - cur
        y = _dot(pooled.astype(BF16), pw_ref[g]) * sc_ref[:, cols]
        o_ref[:, cols] = y.astype(o_ref.dtype)


def _pool(u, pool_w_b, pool_scale):
    S, PW = u.shape
    G, C, _ = pool_w_b.shape
    assert G == len(POOL_WINDOWS) and G * C == PW
    tp = _tile(S, 512, POOL_HALO)
    per = tp // POOL_HALO
    return pl.pallas_call(
        functools.partial(_pool_kernel, tp=tp, group=C), name="pool",
        grid=(S // tp,),
        in_specs=[pl.BlockSpec((tp, PW), lambda i: (i, 0)),
                  pl.BlockSpec((POOL_HALO, PW), lambda i: (jnp.maximum(i * per - 1, 0), 0)),
                  pl.BlockSpec((G, C, C), lambda i: (0, 0, 0)),
                  pl.BlockSpec((1, PW), lambda i: (0, 0))],
        out_specs=pl.BlockSpec((tp, PW), lambda i: (i, 0)),
        out_shape=jax.ShapeDtypeStruct((S, PW), BF16),
        scratch_shapes=[pltpu.VMEM((tp + POOL_HALO, PW), F32)],
        compiler_params=_params("parallel"),
    )(u, u, pool_w_b, pool_scale.reshape(1, PW))


def _moba_kernel(q_ref, k_ref, v_ref, o_ref, kmean_ref, vt_ref, bias_ref, sa_ref, sb_ref,
                 *, nb, group, heads):
    i = pl.program_id(1)
    B = MOBA_BLOCK
    neg = jnp.float32(-jnp.inf)
    U = group
    last = nb // U - 1
    lanes = [slice(h * HEAD_DIM, (h + 1) * HEAD_DIM) for h in range(heads)]

    @pl.when(i == 0)
    def _():
        def prep(g, carry):
            for h in range(heads):
                for u in range(U):
                    rj = pl.multiple_of((g * U + u) * B, B)
                    vt_ref[h, g, :, u * B:(u + 1) * B] = (
                        v_ref[pl.ds(rj, B), lanes[h]].astype(F32).T.astype(BF16))
                    kj = k_ref[pl.ds(rj, B), lanes[h]].astype(F32)
                    kmean_ref[h, pl.ds(g * U + u, 1), :] = jnp.sum(kj, axis=0, keepdims=True) / B
            return carry

        lax.fori_loop(0, nb // U, prep, 0)

    row0 = pl.multiple_of(i * B, B)
    blk = lax.broadcasted_iota(jnp.int32, (nb, B), 0)
    key_id = lax.broadcasted_iota(jnp.int32, (B, B), 0)
    qry_id = lax.broadcasted_iota(jnp.int32, (B, B), 1)
    qts, init = [], []
    for h in range(heads):
        qt = q_ref[:, lanes[h]].astype(F32).T.astype(BF16)
        km = kmean_ref[h]
        km_hi = km.astype(BF16)
        km_lo = (km - km_hi.astype(F32)).astype(BF16)
        gate = _dot(km_hi, qt) + _dot(km_lo, qt)

        g = jnp.where(blk < i, gate, neg)
        bias = jnp.full((nb, B), neg, F32)
        for _ in range(MOBA_TOPK):
            m = jnp.max(g, axis=0, keepdims=True)
            idx = jnp.min(jnp.where(g == m, blk, nb), axis=0, keepdims=True)
            hit = blk == idx
            bias = jnp.where(hit & (m > neg), 0.0, bias)
            g = jnp.where(hit, neg, g)
        bias_ref[h, 0:nb, :] = bias
        bias_ref[h, nb:, :] = jnp.full((U, B), neg, F32)

        s = _dot(k_ref[pl.ds(row0, B), lanes[h]], qt)
        s = jnp.where(key_id <= qry_id, s, neg)
        m0 = jnp.max(s, axis=0, keepdims=True)
        p = jnp.exp2(s - m0)
        l0 = jnp.sum(p, axis=0, keepdims=True)
        vt_own = v_ref[pl.ds(row0, B), lanes[h]].astype(F32).T.astype(BF16)
        acc0 = _dot(vt_own, p.astype(BF16))
        qts.append(qt)
        init.append((m0, l0, acc0))

    def scores(h, g):
        r0 = pl.multiple_of(jnp.minimum(g, last) * (U * B), U * B)
        return _dot(k_ref[pl.ds(r0, U * B), lanes[h]], qts[h])

    def process(h, s_ref, g, state):
        m_run, l_run, acc = state
        sel_u = [bias_ref[h, pl.ds(g * U + u, 1), :] for u in range(U)]
        s_u = [s_ref[h, u * B:(u + 1) * B, :] for u in range(U)]
        m_new = m_run
        for u in range(U):
            m_new = jnp.maximum(m_new, jnp.max(s_u[u], axis=0, keepdims=True) + sel_u[u])
        a = jnp.exp2(m_run - m_new)
        l_new = a * l_run
        acc_new = a * acc
        gv = jnp.minimum(g, last)
        for u in range(U):
            p = jnp.exp2(s_u[u] - (m_new - sel_u[u]))
            l_new = l_new + jnp.sum(p, axis=0, keepdims=True)
            acc_new = acc_new + _dot(vt_ref[h, gv, :, u * B:(u + 1) * B], p.astype(BF16))
        return m_new, l_new, acc_new

    def pair(t, states):
        for h in range(heads):
            sb_ref[h] = scores(h, 2 * t + 1)
        states = tuple(process(h, sa_ref, 2 * t, states[h]) for h in range(heads))
        for h in range(heads):
            sa_ref[h] = scores(h, 2 * t + 2)
        return tuple(process(h, sb_ref, 2 * t + 1, states[h]) for h in range(heads))

    n_groups = (i + U - 1) // U
    for h in range(heads):
        sa_ref[h] = scores(h, 0)
    final = lax.fori_loop(0, (n_groups + 1) // 2, pair, tuple(init))
    for h in range(heads):
        _, l_fin, acc_fin = final[h]
        o_ref[:, lanes[h]] = (acc_fin / l_fin).T.astype(o_ref.dtype)


def _moba(qkv, n_heads):
    S = qkv.shape[0]
    assert S % MOBA_BLOCK == 0
    nb = S // MOBA_BLOCK
    H = n_heads
    group = MOBA_GROUP
    hp = MOBA_HEADS_PER_STEP
    assert nb % group == 0 and H % hp == 0
    G = H // hp
    W = hp * HEAD_DIM
    return pl.pallas_call(
        functools.partial(_moba_kernel, nb=nb, group=group, heads=hp), name="moba",
        grid=(G, nb),
        in_specs=[pl.BlockSpec((MOBA_BLOCK, W), lambda h, i: (i, h)),
                  pl.BlockSpec((S, W), lambda h, i: (0, G + h)),
                  pl.BlockSpec((S, W), lambda h, i: (0, 2 * G + h))],
        out_specs=pl.BlockSpec((MOBA_BLOCK, W), lambda h, i: (i, h)),
        out_shape=jax.ShapeDtypeStruct((S, H * HEAD_DIM), BF16),
        scratch_shapes=[pltpu.VMEM((hp, nb, HEAD_DIM), F32),
                        pltpu.VMEM((hp, nb // group, HEAD_DIM, group * MOBA_BLOCK), BF16),
                        pltpu.VMEM((hp, nb + group, MOBA_BLOCK), F32),
                        pltpu.VMEM((hp, group * MOBA_BLOCK, MOBA_BLOCK), F32),
                        pltpu.VMEM((hp, group * MOBA_BLOCK, MOBA_BLOCK), F32)],
        compiler_params=_params("parallel", "arbitrary"),
    )(qkv, qkv, qkv)


def _mix_up_kernel(p_ref, a_ref, wp_ref, wa_ref, gp_ref, ga_ref, o_ref):
    y_pool = _dot(p_ref[...], wp_ref[...])
    y_attn = _dot(a_ref[...], wa_ref[...])
    o_ref[...] = (gp_ref[...] * y_pool + ga_ref[...] * y_attn).astype(o_ref.dtype)


def _mix_up(p, o, w_up_pool_b, w_up_attn_b, gates):
    S, PW = p.shape
    AW = o.shape[1]
    D = w_up_pool_b.shape[1]
    tm = _tile(S, 512, 8)
    tn = _tile(D, 1024)
    nt = D // tn
    return pl.pallas_call(
        _mix_up_kernel, name="mix_up",
        grid=(S // tm, nt),
        in_specs=[pl.BlockSpec((tm, PW), lambda i, n: (i, 0)),
                  pl.BlockSpec((tm, AW), lambda i, n: (i, 0)),
                  pl.BlockSpec((PW, tn), lambda i, n: (0, n)),
                  pl.BlockSpec((AW, tn), lambda i, n: (0, n)),
                  pl.BlockSpec((tm, tn), lambda i, n: (i, n)),
                  pl.BlockSpec((tm, tn), lambda i, n: (i, nt + n))],
        out_specs=pl.BlockSpec((tm, tn), lambda i, n: (i, n)),
        out_shape=jax.ShapeDtypeStruct((S, D), BF16),
        compiler_params=_params("parallel", "arbitrary"),
    )(p, o, w_up_pool_b, w_up_attn_b, gates, gates)


def _mix_out_kernel(m_ref, w_ref, x_ref, g_ref, b_ref, xf_ref, xb_ref, *, alpha):
    mix = _dot(m_ref[...], w_ref[...])
    y = _layer_norm(alpha * x_ref[...] + mix, g_ref[...], b_ref[...])
    xf_ref[...] = y
    xb_ref[...] = y.astype(BF16)


def _mix_out(m, w_o_b, x, g, b, alpha):
    S, D = x.shape
    tm = _tile(S, 256, 8)
    row = lambda dt: pl.BlockSpec((tm, D), lambda i: (i, 0))
    vec = pl.BlockSpec((1, D), lambda i: (0, 0))
    return pl.pallas_call(
        functools.partial(_mix_out_kernel, alpha=alpha), name="mix_out",
        grid=(S // tm,),
        in_specs=[row(BF16), pl.BlockSpec((D, D), lambda i: (0, 0)), row(F32), vec, vec],
        out_specs=[row(F32), row(BF16)],
        out_shape=[jax.ShapeDtypeStruct((S, D), F32), jax.ShapeDtypeStruct((S, D), BF16)],
        compiler_params=_params("parallel"),
    )(m, w_o_b, x, g.reshape(1, D), b.reshape(1, D))


def _swiglu_kernel(nvalid_ref, eid_ref, x_ref, wg_ref, wu_ref, wd_ref, o_ref, *, sub, tail):
    c = pl.program_id(0)
    f = pl.program_id(1)
    nv = nvalid_ref[c]
    rows = x_ref.shape[0]

    @pl.when(f == 0)
    def _():
        o_ref[...] = jnp.zeros(o_ref.shape, F32)

    def run(n_rows):
        wg = wg_ref[0].astype(BF16)
        wu = wu_ref[0].astype(BF16)
        wd = wd_ref[0].astype(BF16)
        for s0 in range(0, n_rows, sub):
            rs = slice(s0, min(s0 + sub, n_rows))
            xs = x_ref[rs, :]
            gate = _dot(xs, wg)
            up = _dot(xs, wu)
            h = (gate * _sigmoid(gate) * up).astype(BF16)
            o_ref[rs, :] += _dot(h, wd)

    n_var = rows // tail
    for k in range(1, n_var + 1):
        lo = (k - 1) * tail
        cond = (nv > lo) if k == n_var else ((nv > lo) & (nv <= lo + tail))
        pl.when(cond)(functools.partial(run, k * tail))


def _swiglu(xrows, w_gate, w_up, w_down, nvalid, eid, w_base, chunk, tf_pref):
    R, D = xrows.shape
    F = w_gate.shape[-1]
    tf = _tile(F, tf_pref)
    sub = min(EXPERT_SUB, chunk)
    tail = min(EXPERT_TAIL, sub)
    assert R % chunk == 0 and chunk % sub == 0 and sub % tail == 0 and tail % GATHER_ROWS == 0

    def w_in_map(c, f, nv, e):
        return (w_base + e[c], 0, jnp.where(nv[c] > 0, f, 0))

    def w_out_map(c, f, nv, e):
        return (w_base + e[c], jnp.where(nv[c] > 0, f, 0), 0)

    grid_spec = pltpu.PrefetchScalarGridSpec(
        num_scalar_prefetch=2,
        grid=(R // chunk, F // tf),
        in_specs=[pl.BlockSpec((chunk, D), lambda c, f, nv, e: (c, 0)),
                  pl.BlockSpec((1, D, tf), w_in_map),
                  pl.BlockSpec((1, D, tf), w_in_map),
                  pl.BlockSpec((1, tf, D), w_out_map)],
        out_specs=pl.BlockSpec((chunk, D), lambda c, f, nv, e: (c, 0)),
    )
    return pl.pallas_call(
        functools.partial(_swiglu_kernel, sub=sub, tail=tail), name="swiglu",
        grid_spec=grid_spec,
        out_shape=jax.ShapeDtypeStruct((R, D), F32),
        compiler_params=_params("parallel", "arbitrary"),
    )(nvalid, eid, xrows, w_gate, w_up, w_down)


def _router_kernel(x_ref, rh_ref, rl_ref, o_ref):
    x = x_ref[...]
    xh = x.astype(BF16)
    xl = (x - xh.astype(F32)).astype(BF16)
    o_ref[...] = _dot(xh, rh_ref[...]) + _dot(xl, rh_ref[...]) + _dot(xh, rl_ref[...])


def _router(x, router_w):
    S, D = x.shape
    E = router_w.shape[1]
    wpad = jnp.pad(router_w, ((0, 0), (0, LANES - E)))
    rh = wpad.astype(BF16)
    rl = (wpad - rh.astype(F32)).astype(BF16)
    tm = _tile(S, 512, 8)
    wspec = pl.BlockSpec((D, LANES), lambda i: (0, 0))
    logits = pl.pallas_call(
        _router_kernel, name="router",
        grid=(S // tm,),
        in_specs=[pl.BlockSpec((tm, D), lambda i: (i, 0)), wspec, wspec],
        out_specs=pl.BlockSpec((tm, LANES), lambda i: (i, 0)),
        out_shape=jax.ShapeDtypeStruct((S, LANES), F32),
        compiler_params=_params("parallel"),
    )(x, rh, rl)
    return logits[:, :E]


def _gather_kernel(tok_ref, valid_ref, x_hbm, o_ref, buf_ref, sem, *, rows, n_blocks):
    b = pl.program_id(0)

    def start(blk, slot):
        base = blk * rows

        def issue(r, carry):
            tok = tok_ref[base + r]
            pltpu.make_async_copy(x_hbm.at[pl.ds(tok, 1), :], buf_ref.at[slot, pl.ds(r, 1), :],
                                  sem.at[slot]).start()
            return carry

        lax.fori_loop(0, rows, issue, 0, unroll=DMA_ISSUE_UNROLL)

    @pl.when((b == 0) & (valid_ref[0] > 0))
    def _():
        start(0, 0)

    nxt = jnp.minimum(b + 1, n_blocks - 1)

    @pl.when((b + 1 < n_blocks) & (valid_ref[nxt] > 0))
    def _():
        start(nxt, nxt % 2)

    @pl.when(valid_ref[b] > 0)
    def _():
        slot = b % 2
        pltpu.make_async_copy(x_hbm.at[pl.ds(0, rows), :], buf_ref.at[slot], sem.at[slot]).wait()
        o_ref[...] = buf_ref[slot].astype(o_ref.dtype)

    @pl.when(valid_ref[b] == 0)
    def _():
        o_ref[...] = jnp.zeros(o_ref.shape, o_ref.dtype)


def _gather_rows(x, row_tok, blk_valid, n_rows):
    S, D = x.shape
    rows = GATHER_ROWS
    grid_spec = pltpu.PrefetchScalarGridSpec(
        num_scalar_prefetch=2,
        grid=(n_rows // rows,),
        in_specs=[pl.BlockSpec(memory_space=pl.ANY)],
        out_specs=pl.BlockSpec((rows, D), lambda b, t, v: (b, 0)),
        scratch_shapes=[pltpu.VMEM((2, rows, D), F32), pltpu.SemaphoreType.DMA((2,))],
    )
    return pl.pallas_call(
        functools.partial(_gather_kernel, rows=rows, n_blocks=n_rows // rows), name="gather_rows",
        grid_spec=grid_spec,
        out_shape=jax.ShapeDtypeStruct((n_rows, D), BF16),
        compiler_params=_params("arbitrary"),
    )(row_tok, blk_valid, x)


def _combine_kernel(pos_ref, y_hbm, w_ref, x_ref, g_ref, b_ref, xf_ref, xb_ref, buf_ref, sem,
                    *, rows, n_blocks, alpha):
    b = pl.program_id(0)

    def start(blk, slot):
        base = blk * rows * MOE_TOPK

        def issue(r, carry):
            for k in range(MOE_TOPK):
                src = pos_ref[base + r * MOE_TOPK + k]
                pltpu.make_async_copy(y_hbm.at[pl.ds(src, 1), :], buf_ref.at[slot, k, pl.ds(r, 1), :],
                                      sem.at[slot]).start()
            return carry

        lax.fori_loop(0, rows, issue, 0, unroll=DMA_ISSUE_UNROLL)

    @pl.when(b == 0)
    def _():
        start(0, 0)

    @pl.when(b + 1 < n_blocks)
    def _():
        start(b + 1, (b + 1) % 2)

    slot = b % 2
    for k in range(MOE_TOPK):
        pltpu.make_async_copy(y_hbm.at[pl.ds(0, rows), :], buf_ref.at[slot, k], sem.at[slot]).wait()
    w = w_ref[...]
    f = w[:, 0:1] * buf_ref[slot, 0] + w[:, 1:2] * buf_ref[slot, 1]
    y = _layer_norm(alpha * x_ref[...] + f, g_ref[...], b_ref[...])
    xf_ref[...] = y
    xb_ref[...] = y.astype(BF16)


def _combine_ln(y, pos, top_w, x, g, b, alpha):
    S, D = x.shape
    rows = _tile(S, GATHER_ROWS, 8)
    row = pl.BlockSpec((rows, D), lambda i, p: (i, 0))
    vec = pl.BlockSpec((1, D), lambda i, p: (0, 0))
    grid_spec = pltpu.PrefetchScalarGridSpec(
        num_scalar_prefetch=1,
        grid=(S // rows,),
        in_specs=[pl.BlockSpec(memory_space=pl.ANY),
                  pl.BlockSpec((rows, MOE_TOPK), lambda i, p: (i, 0)),
                  row, vec, vec],
        out_specs=[row, row],
        scratch_shapes=[pltpu.VMEM((2, MOE_TOPK, rows, D), F32), pltpu.SemaphoreType.DMA((2,))],
    )
    return pl.pallas_call(
        functools.partial(_combine_kernel, rows=rows, n_blocks=S // rows, alpha=alpha), name="combine_ln",
        grid_spec=grid_spec,
        out_shape=[jax.ShapeDtypeStruct((S, D), F32), jax.ShapeDtypeStruct((S, D), BF16)],
        compiler_params=_params("arbitrary"),
    )(pos, y, top_w, x, g.reshape(1, D), b.reshape(1, D))


def _route(logits, chunk):
    S, E = logits.shape
    top_logit, top_e = lax.top_k(logits, MOE_TOPK)
    top_w = jax.nn.softmax(top_logit, axis=-1)
    n_assign = S * MOE_TOPK
    flat_e = top_e.reshape(-1).astype(jnp.int32)
    onehot = (flat_e[:, None] == jnp.arange(E, dtype=jnp.int32)[None, :]).astype(jnp.int32)
    running = jnp.cumsum(onehot, axis=0)
    rank = jnp.sum(running * onehot, axis=1) - 1
    counts = running[-1]
    padded = (counts + chunk - 1) // chunk * chunk
    ends = jnp.cumsum(padded)
    pstart = ends - padded
    dest = pstart[flat_e] + rank
    n_rows = (-(-n_assign // chunk) + E) * chunk
    n_chunks = n_rows // chunk
    flat_tok = jnp.arange(n_assign, dtype=jnp.int32) // MOE_TOPK
    row_tok = jnp.zeros((n_rows,), jnp.int32).at[dest].set(flat_tok)
    chunk_start = jnp.arange(n_chunks, dtype=jnp.int32) * chunk
    eid = jnp.minimum(jnp.searchsorted(ends, chunk_start, side="right"), E - 1).astype(jnp.int32)
    nvalid = jnp.clip(counts[eid] - (chunk_start - pstart[eid]), 0, chunk).astype(jnp.int32)
    per = chunk // GATHER_ROWS
    blk = jnp.arange(n_rows // GATHER_ROWS, dtype=jnp.int32)
    blk_valid = (nvalid[blk // per] > (blk % per) * GATHER_ROWS).astype(jnp.int32)
    return top_w, dest.astype(jnp.int32), row_tok, eid, nvalid, blk_valid, n_rows


def kernel(x, ln_in_g, ln_in_b, w_in, pool_w, pool_scale, w_up_pool, w_up_attn, w_o, ln_mix_g, ln_mix_b, ffn_w_gate, ffn_w_up, ffn_w_down, moe_router, moe_w_gate, moe_w_up, moe_w_down, ln_ffn_g, ln_ffn_b):
    B, S, D = x.shape
    assert B == 1
    depth = w_in.shape[0]
    PW = w_up_pool.shape[1]
    AW = w_up_attn.shape[1]
    n_heads = AW // HEAD_DIM
    n_exp = moe_router.shape[-1]
    alpha = float((2 * depth) ** 0.25)

    pool_w_b = pool_w.astype(BF16)
    w_up_pool_b = w_up_pool.astype(BF16)
    w_up_attn_b = w_up_attn.astype(BF16)
    w_o_b = w_o.astype(BF16)
    moe_gate = moe_w_gate.reshape((-1,) + moe_w_gate.shape[2:])
    moe_up = moe_w_up.reshape((-1,) + moe_w_up.shape[2:])
    moe_down = moe_w_down.reshape((-1,) + moe_w_down.shape[2:])

    chunk = min(EXPERT_CHUNK, S)
    dense_nvalid = jnp.full((S // chunk,), chunk, jnp.int32)
    dense_eid = jnp.zeros((S // chunk,), jnp.int32)

    xf, xb = _ln_in(x.reshape(S, D), ln_in_g, ln_in_b)
    for l in range(depth):
        u = _proj(xb, w_in, l, 0, PW, F32, "plain")
        qkv = _proj(xb, w_in, l, PW, 3 * AW, BF16, "qkv", q_width=AW)
        gates = _proj(xb, w_in, l, PW + 3 * AW, 2 * D, F32, "sigmoid")
        p = _pool(u, pool_w_b[l], pool_scale[l])
        o = _moba(qkv, n_heads)
        m = _mix_up(p, o, w_up_pool_b[l], w_up_attn_b[l], gates)
        xf, xb = _mix_out(m, w_o_b[l], xf, ln_mix_g[l], ln_mix_b[l], alpha)
        i = l // 2
        if l % 2 == 0:
            f = _swiglu(xb, ffn_w_gate, ffn_w_up, ffn_w_down, dense_nvalid, dense_eid, i, chunk, 512)
            xf, xb = _add_ln(xf, f, ln_ffn_g[l], ln_ffn_b[l], alpha)
        else:
            logits = _router(xf, moe_router[i])
            top_w, dest, row_tok, eid, nvalid, blk_valid, n_rows = _route(logits, chunk)
            xg = _gather_rows(xf, row_tok, blk_valid, n_rows)
            y = _swiglu(xg, moe_gate, moe_up, moe_down, nvalid, eid, i * n_exp, chunk, 512)
            xf, xb = _combine_ln(y, dest, top_w, xf, ln_ffn_g[l], ln_ffn_b[l], alpha)
    return xf.reshape(B, S, D)
```

```python
import functools

import jax
import jax.numpy as jnp
import numpy as np
from jax import lax
from jax.experimental import pallas as pl
from jax.experimental.pallas import tpu as pltpu

F32 = jnp.float32
BF16 = jnp.bfloat16

HEAD_DIM = 128
MOBA_BLOCK = 256
MOBA_TOPK = 3
MOBA_GROUP = 2
MOBA_HEADS_PER_STEP = 2
POOL_WINDOWS = (2, 4, 8, 16)
POOL_HALO = 16
MOE_TOPK = 2
LN_EPS = 1e-5
LOG2_E = float(np.log2(np.e))
LANES = 128
VMEM_LIMIT = 56 * 1024 * 1024

EXPERT_CHUNK = 1024
EXPERT_SUB = 512
EXPERT_TAIL = 256
GATHER_ROWS = 256
DMA_ISSUE_UNROLL = 8


def _tile(n, pref, mult=LANES):
    if n <= pref:
        return n
    t = (pref // mult) * mult
    while t > mult and n % t:
        t -= mult
    assert n % t == 0, (n, pref)
    return t


def _params(*sem):
    return pltpu.CompilerParams(dimension_semantics=sem, vmem_limit_bytes=VMEM_LIMIT)


def _layer_norm(v, g, b):
    mu = jnp.mean(v, axis=-1, keepdims=True)
    d = v - mu
    var = jnp.mean(d * d, axis=-1, keepdims=True)
    return d * lax.rsqrt(var + LN_EPS) * g + b


def _sigmoid(v):
    return 1.0 / (1.0 + jnp.exp(-v))


def _dot(a, b):
    return jnp.dot(a, b, preferred_element_type=F32)


def _dot_nt(a, b):
    return lax.dot_general(a, b, (((1,), (1,)), ((), ())), preferred_element_type=F32)


def _ln_kernel(x_ref, g_ref, b_ref, xf_ref, xb_ref):
    y = _layer_norm(x_ref[...], g_ref[...], b_ref[...])
    xf_ref[...] = y
    xb_ref[...] = y.astype(BF16)


def _ln_in(x, g, b):
    S, D = x.shape
    tm = _tile(S, 512, 8)
    row = pl.BlockSpec((tm, D), lambda i: (i, 0))
    vec = pl.BlockSpec((1, D), lambda i: (0, 0))
    return pl.pallas_call(
        _ln_kernel, name="ln_in",
        grid=(S // tm,),
        in_specs=[row, vec, vec],
        out_specs=[row, row],
        out_shape=[jax.ShapeDtypeStruct((S, D), F32), jax.ShapeDtypeStruct((S, D), BF16)],
        compiler_params=_params("parallel"),
    )(x, g.reshape(1, D), b.reshape(1, D))


def _add_ln_kernel(x_ref, f_ref, g_ref, b_ref, xf_ref, xb_ref, *, alpha):
    y = _layer_norm(alpha * x_ref[...] + f_ref[...], g_ref[...], b_ref[...])
    xf_ref[...] = y
    xb_ref[...] = y.astype(BF16)


def _add_ln(x, f, g, b, alpha):
    S, D = x.shape
    tm = _tile(S, 512, 8)
    row = pl.BlockSpec((tm, D), lambda i: (i, 0))
    vec = pl.BlockSpec((1, D), lambda i: (0, 0))
    return pl.pallas_call(
        functools.partial(_add_ln_kernel, alpha=alpha), name="add_ln",
        grid=(S // tm,),
        in_specs=[row, row, vec, vec],
        out_specs=[row, row],
        out_shape=[jax.ShapeDtypeStruct((S, D), F32), jax.ShapeDtypeStruct((S, D), BF16)],
        compiler_params=_params("parallel"),
    )(x, f, g.reshape(1, D), b.reshape(1, D))


def _proj_kernel(a_ref, b_ref, o_ref, bb_ref, *, mode, q_tiles, q_scale):
    @pl.when(pl.program_id(1) == 0)
    def _():
        bb_ref[...] = b_ref[0].astype(BF16)

    acc = _dot(a_ref[...], bb_ref[...])
    if mode == "qkv":
        acc = acc * jnp.where(pl.program_id(0) < q_tiles, q_scale, 1.0).astype(F32)
    elif mode == "sigmoid":
        acc = _sigmoid(acc)
    o_ref[...] = acc.astype(o_ref.dtype)


def _proj(xb, w_in, layer, col0, width, out_dtype, mode, q_width=0):
    S, D = xb.shape
    tm = _tile(S, 1024, 8)
    tn = _tile(int(np.gcd.reduce([width, col0 or width, q_width or width])), 1024)
    assert col0 % tn == 0 and width % tn == 0 and q_width % tn == 0
    off = col0 // tn
    kern = functools.partial(_proj_kernel, mode=mode, q_tiles=q_width // tn,
                             q_scale=HEAD_DIM ** -0.5 * LOG2_E)
    return pl.pallas_call(
        kern, name="proj_" + mode,
        grid=(width // tn, S // tm),
        in_specs=[pl.BlockSpec((tm, D), lambda n, i: (i, 0)),
                  pl.BlockSpec((1, D, tn), lambda n, i: (layer, 0, off + n))],
        out_specs=pl.BlockSpec((tm, tn), lambda n, i: (i, n)),
        out_shape=jax.ShapeDtypeStruct((S, width), out_dtype),
        scratch_shapes=[pltpu.VMEM((D, tn), BF16)],
        compiler_params=_params("parallel", "arbitrary"),
    )(xb, w_in)


def _pool_kernel(u_ref, uprev_ref, pw_ref, sc_ref, o_ref, ext_ref, *, tp, group):
    i = pl.program_id(0)
    halo = uprev_ref[...]
    ext_ref[0:POOL_HALO, :] = jnp.where(i > 0, halo, jnp.zeros_like(halo))
    ext_ref[POOL_HALO:, :] = u_ref[...]
    t = i * tp + lax.broadcasted_iota(jnp.int32, (tp, 1), 0)
    for g, w in enumerate(POOL_WINDOWS):
        cols = slice(g * group, (g + 1) * group)
        s = ext_ref[:, cols]
        k = 1
        while k < w:
            s = s + pltpu.roll(s, k, 0)
            k *= 2
        cnt = jnp.minimum(t + 1, w).astype(F32)
        cur = u_ref[:, cols]
        pooled = s[POOL_HALO:, :] / cnt - cur
        y = _dot(pooled.astype(BF16), pw_ref[g]) * sc_ref[:, cols]
        o_ref[:, cols] = y.astype(o_ref.dtype)


def _pool(u, pool_w_b, pool_scale):
    S, PW = u.shape
    G, C, _ = pool_w_b.shape
    assert G == len(POOL_WINDOWS) and G * C == PW
    tp = _tile(S, 512, POOL_HALO)
    per = tp // POOL_HALO
    return pl.pallas_call(
        functools.partial(_pool_kernel, tp=tp, group=C), name="pool",
        grid=(S // tp,),
        in_specs=[pl.BlockSpec((tp, PW), lambda i: (i, 0)),
                  pl.BlockSpec((POOL_HALO, PW), lambda i: (jnp.maximum(i * per - 1, 0), 0)),
                  pl.BlockSpec((G, C, C), lambda i: (0, 0, 0)),
                  pl.BlockSpec((1, PW), lambda i: (0, 0))],
        out_specs=pl.BlockSpec((tp, PW), lambda i: (i, 0)),
        out_shape=jax.ShapeDtypeStruct((S, PW), BF16),
        scratch_shapes=[pltpu.VMEM((tp + POOL_HALO, PW), F32)],
        compiler_params=_params("parallel"),
    )(u, u, pool_w_b, pool_scale.reshape(1, PW))


def _moba_kernel(q_ref, k_ref, v_ref, o_ref, kmean_ref, vt_ref, bias_ref, sa_ref, sb_ref,
                 *, nb, group, heads):
    i = pl.program_id(1)
    B = MOBA_BLOCK
    neg = jnp.float32(-jnp.inf)
    U = group
    last = nb // U - 1
    lanes = [slice(h * HEAD_DIM, (h + 1) * HEAD_DIM) for h in range(heads)]

    @pl.when(i == 0)
    def _():
        def prep(g, carry):
            for h in range(heads):
                for u in range(U):
                    rj = pl.multiple_of((g * U + u) * B, B)
                    vt_ref[h, g, :, u * B:(u + 1) * B] = (
                        v_ref[pl.ds(rj, B), lanes[h]].astype(F32).T.astype(BF16))
                    kj = k_ref[pl.ds(rj, B), lanes[h]].astype(F32)
                    kmean_ref[h, pl.ds(g * U + u, 1), :] = jnp.sum(kj, axis=0, keepdims=True) / B
            return carry

        lax.fori_loop(0, nb // U, prep, 0)

    row0 = pl.multiple_of(i * B, B)
    blk = lax.broadcasted_iota(jnp.int32, (nb, B), 0)
    key_id = lax.broadcasted_iota(jnp.int32, (B, B), 0)
    qry_id = lax.broadcasted_iota(jnp.int32, (B, B), 1)
    qts = [q_ref[:, lanes[h]].astype(F32).T.astype(BF16) for h in range(heads)]

    def scores(h, g):
        r0 = pl.multiple_of(jnp.minimum(g, last) * (U * B), U * B)
        return _dot(k_ref[pl.ds(r0, U * B), lanes[h]], qts[h])

    gates, s_own, vt_own = [], [], []
    for h in range(heads):
        km = kmean_ref[h]
        km_hi = km.astype(BF16)
        km_lo = (km - km_hi.astype(F32)).astype(BF16)
        gates.append(_dot(km_hi, qts[h]) + _dot(km_lo, qts[h]))
        s_own.append(_dot(k_ref[pl.ds(row0, B), lanes[h]], qts[h]))
        vt_own.append(v_ref[pl.ds(row0, B), lanes[h]].astype(F32).T.astype(BF16))
    for h in range(heads):
        sa_ref[h] = scores(h, 0)

    init = []
    for h in range(heads):
        g = jnp.where(blk < i, gates[h], neg)
        bias = jnp.full((nb, B), neg, F32)
        for _ in range(MOBA_TOPK):
            m = jnp.max(g, axis=0, keepdims=True)
            idx = jnp.min(jnp.where(g == m, blk, nb), axis=0, keepdims=True)
            hit = blk == idx
            bias = jnp.where(hit & (m > neg), 0.0, bias)
            g = jnp.where(hit, neg, g)
        bias_ref[h, 0:nb, :] = bias
        bias_ref[h, nb:, :] = jnp.full((U, B), neg, F32)

        s = jnp.where(key_id <= qry_id, s_own[h], neg)
        m0 = jnp.max(s, axis=0, keepdims=True)
        p = jnp.exp2(s - m0)
        l0 = jnp.sum(p, axis=0, keepdims=True)
        init.append((m0, l0, p.astype(BF16)))
    init = [(m0, l0, _dot(vt_own[h], p)) for h, (m0, l0, p) in enumerate(init)]

    def process(h, s_ref, g, state):
        m_run, l_run, acc = state
        sel_u = [bias_ref[h, pl.ds(g * U + u, 1), :] for u in range(U)]
        s_u = [s_ref[h, u * B:(u + 1) * B, :] for u in range(U)]
        m_new = m_run
        for u in range(U):
            m_new = jnp.maximum(m_new, jnp.max(s_u[u], axis=0, keepdims=True) + sel_u[u])
        a = jnp.exp2(m_run - m_new)
        l_new = a * l_run
        acc_new = a * acc
        gv = jnp.minimum(g, last)
        for u in range(U):
            p = jnp.exp2(s_u[u] - (m_new - sel_u[u]))
            l_new = l_new + jnp.sum(p, axis=0, keepdims=True)
            acc_new = acc_new + _dot(vt_ref[h, gv, :, u * B:(u + 1) * B], p.astype(BF16))
        return m_new, l_new, acc_new

    def pair(t, states):
        for h in range(heads):
            sb_ref[h] = scores(h, 2 * t + 1)
        states = tuple(process(h, sa_ref, 2 * t, states[h]) for h in range(heads))
        for h in range(heads):
            sa_ref[h] = scores(h, 2 * t + 2)
        return tuple(process(h, sb_ref, 2 * t + 1, states[h]) for h in range(heads))

    n_groups = (i + U - 1) // U
    final = lax.fori_loop(0, (n_groups + 1) // 2, pair, tuple(init))
    for h in range(heads):
        _, l_fin, acc_fin = final[h]
        o_ref[:, lanes[h]] = (acc_fin / l_fin).T.astype(o_ref.dtype)


def _moba(qkv, n_heads):
    S = qkv.shape[0]
    assert S % MOBA_BLOCK == 0
    nb = S // MOBA_BLOCK
    H = n_heads
    group = MOBA_GROUP
    hp = MOBA_HEADS_PER_STEP
    assert nb % group == 0 and H % hp == 0
    G = H // hp
    W = hp * HEAD_DIM
    return pl.pallas_call(
        functools.partial(_moba_kernel, nb=nb, group=group, heads=hp), name="moba",
        grid=(G, nb),
        in_specs=[pl.BlockSpec((MOBA_BLOCK, W), lambda h, i: (i, h)),
                  pl.BlockSpec((S, W), lambda h, i: (0, G + h)),
                  pl.BlockSpec((S, W), lambda h, i: (0, 2 * G + h))],
        out_specs=pl.BlockSpec((MOBA_BLOCK, W), lambda h, i: (i, h)),
        out_shape=jax.ShapeDtypeStruct((S, H * HEAD_DIM), BF16),
        scratch_shapes=[pltpu.VMEM((hp, nb, HEAD_DIM), F32),
                        pltpu.VMEM((hp, nb // group, HEAD_DIM, group * MOBA_BLOCK), BF16),
                        pltpu.VMEM((hp, nb + group, MOBA_BLOCK), F32),
                        pltpu.VMEM((hp, group * MOBA_BLOCK, MOBA_BLOCK), F32),
                        pltpu.VMEM((hp, group * MOBA_BLOCK, MOBA_BLOCK), F32)],
        compiler_params=_params("parallel", "arbitrary"),
    )(qkv, qkv, qkv)


def _mix_up_kernel(p_ref, a_ref, wp_ref, wa_ref, gp_ref, ga_ref, o_ref):
    y_pool = _dot(p_ref[...], wp_ref[...])
    y_attn = _dot(a_ref[...], wa_ref[...])
    o_ref[...] = (gp_ref[...] * y_pool + ga_ref[...] * y_attn).astype(o_ref.dtype)


def _mix_up(p, o, w_up_pool_b, w_up_attn_b, gates):
    S, PW = p.shape
    AW = o.shape[1]
    D = w_up_pool_b.shape[1]
    tm = _tile(S, 1024, 8)
    tn = _tile(D, 1024)
    nt = D // tn
    return pl.pallas_call(
        _mix_up_kernel, name="mix_up",
        grid=(S // tm, nt),
        in_specs=[pl.BlockSpec((tm, PW), lambda i, n: (i, 0)),
                  pl.BlockSpec((tm, AW), lambda i, n: (i, 0)),
                  pl.BlockSpec((PW, tn), lambda i, n: (0, n)),
                  pl.BlockSpec((AW, tn), lambda i, n: (0, n)),
                  pl.BlockSpec((tm, tn), lambda i, n: (i, n)),
                  pl.BlockSpec((tm, tn), lambda i, n: (i, nt + n))],
        out_specs=pl.BlockSpec((tm, tn), lambda i, n: (i, n)),
        out_shape=jax.ShapeDtypeStruct((S, D), BF16),
        compiler_params=_params("parallel", "arbitrary"),
    )(p, o, w_up_pool_b, w_up_attn_b, gates, gates)


def _mix_out_kernel(m_ref, w_ref, x_ref, g_ref, b_ref, xf_ref, xb_ref, *, alpha):
    mix = _dot(m_ref[...], w_ref[...])
    y = _layer_norm(alpha * x_ref[...] + mix, g_ref[...], b_ref[...])
    xf_ref[...] = y
    xb_ref[...] = y.astype(BF16)


def _mix_out(m, w_o_b, x, g, b, alpha):
    S, D = x.shape
    tm = _tile(S, 512, 8)
    row = lambda dt: pl.BlockSpec((tm, D), lambda i: (i, 0))
    vec = pl.BlockSpec((1, D), lambda i: (0, 0))
    return pl.pallas_call(
        functools.partial(_mix_out_kernel, alpha=alpha), name="mix_out",
        grid=(S // tm,),
        in_specs=[row(BF16), pl.BlockSpec((D, D), lambda i: (0, 0)), row(F32), vec, vec],
        out_specs=[row(F32), row(BF16)],
        out_shape=[jax.ShapeDtypeStruct((S, D), F32), jax.ShapeDtypeStruct((S, D), BF16)],
        compiler_params=_params("parallel"),
    )(m, w_o_b, x, g.reshape(1, D), b.reshape(1, D))


def _swiglu_kernel(nvalid_ref, eid_ref, x_ref, wg_ref, wu_ref, wd_ref, o_ref, *, sub, tail):
    c = pl.program_id(0)
    f = pl.program_id(1)
    nv = nvalid_ref[c]
    rows = x_ref.shape[0]

    @pl.when(f == 0)
    def _():
        o_ref[...] = jnp.zeros(o_ref.shape, F32)

    def run(n_rows):
        wg = wg_ref[0].astype(BF16)
        wu = wu_ref[0].astype(BF16)
        wd = wd_ref[0].astype(BF16)
        for s0 in range(0, n_rows, sub):
            rs = slice(s0, min(s0 + sub, n_rows))
            xs = x_ref[rs, :]
            gate = _dot(xs, wg)
            up = _dot(xs, wu)
            h = (gate * _sigmoid(gate) * up).astype(BF16)
            o_ref[rs, :] += _dot(h, wd)

    n_var = rows // tail
    for k in range(1, n_var + 1):
        lo = (k - 1) * tail
        cond = (nv > lo) if k == n_var else ((nv > lo) & (nv <= lo + tail))
        pl.when(cond)(functools.partial(run, k * tail))


def _swiglu(xrows, w_gate, w_up, w_down, nvalid, eid, w_base, chunk, tf_pref):
    R, D = xrows.shape
    F = w_gate.shape[-1]
    tf = _tile(F, tf_pref)
    sub = min(EXPERT_SUB, chunk)
    tail = min(EXPERT_TAIL, sub)
    assert R % chunk == 0 and chunk % sub == 0 and sub % tail == 0 and tail % GATHER_ROWS == 0

    def w_in_map(c, f, nv, e):
        return (w_base + e[c], 0, jnp.where(nv[c] > 0, f, 0))

    def w_out_map(c, f, nv, e):
        return (w_base + e[c], jnp.where(nv[c] > 0, f, 0), 0)

    grid_spec = pltpu.PrefetchScalarGridSpec(
        num_scalar_prefetch=2,
        grid=(R // chunk, F // tf),
        in_specs=[pl.BlockSpec((chunk, D), lambda c, f, nv, e: (c, 0)),
                  pl.BlockSpec((1, D, tf), w_in_map),
                  pl.BlockSpec((1, D, tf), w_in_map),
                  pl.BlockSpec((1, tf, D), w_out_map)],
        out_specs=pl.BlockSpec((chunk, D), lambda c, f, nv, e: (c, 0)),
    )
    return pl.pallas_call(
        functools.partial(_swiglu_kernel, sub=sub, tail=tail), name="swiglu",
        grid_spec=grid_spec,
        out_shape=jax.ShapeDtypeStruct((R, D), F32),
        compiler_params=_params("parallel", "arbitrary"),
    )(nvalid, eid, xrows, w_gate, w_up, w_down)


def _router_kernel(x_ref, rh_ref, rl_ref, o_ref):
    x = x_ref[...]
    xh = x.astype(BF16)
    xl = (x - xh.astype(F32)).astype(BF16)
    o_ref[...] = _dot(xh, rh_ref[...]) + _dot(xl, rh_ref[...]) + _dot(xh, rl_ref[...])


def _router(x, router_w):
    S, D = x.shape
    E = router_w.shape[1]
    wpad = jnp.pad(router_w, ((0, 0), (0, LANES - E)))
    rh = wpad.astype(BF16)
    rl = (wpad - rh.astype(F32)).astype(BF16)
    tm = _tile(S, 512, 8)
    wspec = pl.BlockSpec((D, LANES), lambda i: (0, 0))
    logits = pl.pallas_call(
        _router_kernel, name="router",
        grid=(S // tm,),
        in_specs=[pl.BlockSpec((tm, D), lambda i: (i, 0)), wspec, wspec],
        out_specs=pl.BlockSpec((tm, LANES), lambda i: (i, 0)),
        out_shape=jax.ShapeDtypeStruct((S, LANES), F32),
        compiler_params=_params("parallel"),
    )(x, rh, rl)
    return logits[:, :E]


def _gather_kernel(tok_ref, valid_ref, x_hbm, o_ref, buf_ref, sem, *, rows, n_blocks):
    b = pl.program_id(0)

    def start(blk, slot):
        base = blk * rows

        def issue(r, carry):
            tok = tok_ref[base + r]
            pltpu.make_async_copy(x_hbm.at[pl.ds(tok, 1), :], buf_ref.at[slot, pl.ds(r, 1), :],
                                  sem.at[slot]).start()
            return carry

        lax.fori_loop(0, rows, issue, 0, unroll=DMA_ISSUE_UNROLL)

    @pl.when((b == 0) & (valid_ref[0] > 0))
    def _():
        start(0, 0)

    nxt = jnp.minimum(b + 1, n_blocks - 1)

    @pl.when((b + 1 < n_blocks) & (valid_ref[nxt] > 0))
    def _():
        start(nxt, nxt % 2)

    @pl.when(valid_ref[b] > 0)
    def _():
        slot = b % 2
        pltpu.make_async_copy(x_hbm.at[pl.ds(0, rows), :], buf_ref.at[slot], sem.at[slot]).wait()
        o_ref[...] = buf_ref[slot].astype(o_ref.dtype)

    @pl.when(valid_ref[b] == 0)
    def _():
        o_ref[...] = jnp.zeros(o_ref.shape, o_ref.dtype)


def _gather_rows(x, row_tok, blk_valid, n_rows):
    S, D = x.shape
    rows = GATHER_ROWS
    grid_spec = pltpu.PrefetchScalarGridSpec(
        num_scalar_prefetch=2,
        grid=(n_rows // rows,),
        in_specs=[pl.BlockSpec(memory_space=pl.ANY)],
        out_specs=pl.BlockSpec((rows, D), lambda b, t, v: (b, 0)),
        scratch_shapes=[pltpu.VMEM((2, rows, D), F32), pltpu.SemaphoreType.DMA((2,))],
    )
    return pl.pallas_call(
        functools.partial(_gather_kernel, rows=rows, n_blocks=n_rows // rows), name="gather_rows",
        grid_spec=grid_spec,
        out_shape=jax.ShapeDtypeStruct((n_rows, D), BF16),
        compiler_params=_params("arbitrary"),
    )(row_tok, blk_valid, x)


def _combine_kernel(pos_ref, y_hbm, w_ref, x_ref, g_ref, b_ref, xf_ref, xb_ref, buf_ref, sem,
                    *, rows, n_blocks, alpha):
    b = pl.program_id(0)

    def start(blk, slot):
        base = blk * rows * MOE_TOPK

        def issue(r, carry):
            for k in range(MOE_TOPK):
                src = pos_ref[base + r * MOE_TOPK + k]
                pltpu.make_async_copy(y_hbm.at[pl.ds(src, 1), :], buf_ref.at[slot, k, pl.ds(r, 1), :],
                                      sem.at[slot]).start()
            return carry

        lax.fori_loop(0, rows, issue, 0, unroll=DMA_ISSUE_UNROLL)

    @pl.when(b == 0)
    def _():
        start(0, 0)

    @pl.when(b + 1 < n_blocks)
    def _():
        start(b + 1, (b + 1) % 2)

    slot = b % 2
    for k in range(MOE_TOPK):
        pltpu.make_async_copy(y_hbm.at[pl.ds(0, rows), :], buf_ref.at[slot, k], sem.at[slot]).wait()
    w = w_ref[...]
    f = w[:, 0:1] * buf_ref[slot, 0] + w[:, 1:2] * buf_ref[slot, 1]
    y = _layer_norm(alpha * x_ref[...] + f, g_ref[...], b_ref[...])
    xf_ref[...] = y
    xb_ref[...] = y.astype(BF16)


def _combine_ln(y, pos, top_w, x, g, b, alpha):
    S, D = x.shape
    rows = _tile(S, GATHER_ROWS, 8)
    row = pl.BlockSpec((rows, D), lambda i, p: (i, 0))
    vec = pl.BlockSpec((1, D), lambda i, p: (0, 0))
    grid_spec = pltpu.PrefetchScalarGridSpec(
        num_scalar_prefetch=1,
        grid=(S // rows,),
        in_specs=[pl.BlockSpec(memory_space=pl.ANY),
                  pl.BlockSpec((rows, MOE_TOPK), lambda i, p: (i, 0)),
                  row, vec, vec],
        out_specs=[row, row],
        scratch_shapes=[pltpu.VMEM((2, MOE_TOPK, rows, D), F32), pltpu.SemaphoreType.DMA((2,))],
    )
    return pl.pallas_call(
        functools.partial(_combine_kernel, rows=rows, n_blocks=S // rows, alpha=alpha), name="combine_ln",
        grid_spec=grid_spec,
        out_shape=[jax.ShapeDtypeStruct((S, D), F32), jax.ShapeDtypeStruct((S, D), BF16)],
        compiler_params=_params("arbitrary"),
    )(pos, y, top_w, x, g.reshape(1, D), b.reshape(1, D))


def _route(logits, chunk):
    S, E = logits.shape
    top_logit, top_e = lax.top_k(logits, MOE_TOPK)
    top_w = jax.nn.softmax(top_logit, axis=-1)
    n_assign = S * MOE_TOPK
    flat_e = top_e.reshape(-1).astype(jnp.int32)
    onehot = (flat_e[:, None] == jnp.arange(E, dtype=jnp.int32)[None, :]).astype(jnp.int32)
    running = jnp.cumsum(onehot, axis=0)
    rank = jnp.sum(running * onehot, axis=1) - 1
    counts = running[-1]
    padded = (counts + chunk - 1) // chunk * chunk
    ends = jnp.cumsum(padded)
    pstart = ends - padded
    dest = pstart[flat_e] + rank
    n_rows = (-(-n_assign // chunk) + E) * chunk
    n_chunks = n_rows // chunk
    flat_tok = jnp.arange(n_assign, dtype=jnp.int32) // MOE_TOPK
    row_tok = jnp.zeros((n_rows,), jnp.int32).at[dest].set(flat_tok)
    chunk_start = jnp.arange(n_chunks, dtype=jnp.int32) * chunk
    eid = jnp.minimum(jnp.searchsorted(ends, chunk_start, side="right"), E - 1).astype(jnp.int32)
    nvalid = jnp.clip(counts[eid] - (chunk_start - pstart[eid]), 0, chunk).astype(jnp.int32)
    per = chunk // GATHER_ROWS
    blk = jnp.arange(n_rows // GATHER_ROWS, dtype=jnp.int32)
    blk_valid = (nvalid[blk // per] > (blk % per) * GATHER_ROWS).astype(jnp.int32)
    return top_w, dest.astype(jnp.int32), row_tok, eid, nvalid, blk_valid, n_rows


def kernel(x, ln_in_g, ln_in_b, w_in, pool_w, pool_scale, w_up_pool, w_up_attn, w_o, ln_mix_g, ln_mix_b, ffn_w_gate, ffn_w_up, ffn_w_down, moe_router, moe_w_gate, moe_w_up, moe_w_down, ln_ffn_g, ln_ffn_b):
    B, S, D = x.shape
    assert B == 1
    depth = w_in.shape[0]
    PW = w_up_pool.shape[1]
    AW = w_up_attn.shape[1]
    n_heads = AW // HEAD_DIM
    n_exp = moe_router.shape[-1]
    alpha = float((2 * depth) ** 0.25)

    pool_w_b = pool_w.astype(BF16)
    w_up_pool_b = w_up_pool.astype(BF16)
    w_up_attn_b = w_up_attn.astype(BF16)
    w_o_b = w_o.astype(BF16)
    moe_gate = moe_w_gate.reshape((-1,) + moe_w_gate.shape[2:])
    moe_up = moe_w_up.reshape((-1,) + moe_w_up.shape[2:])
    moe_down = moe_w_down.reshape((-1,) + moe_w_down.shape[2:])

    chunk = min(EXPERT_CHUNK, S)
    dense_nvalid = jnp.full((S // chunk,), chunk, jnp.int32)
    dense_eid = jnp.zeros((S // chunk,), jnp.int32)

    xf, xb = _ln_in(x.reshape(S, D), ln_in_g, ln_in_b)
    for l in range(depth):
        u = _proj(xb, w_in, l, 0, PW, F32, "plain")
        qkv = _proj(xb, w_in, l, PW, 3 * AW, BF16, "qkv", q_width=AW)
        gates = _proj(xb, w_in, l, PW + 3 * AW, 2 * D, F32, "sigmoid")
        p = _pool(u, pool_w_b[l], pool_scale[l])
        o = _moba(qkv, n_heads)
        m = _mix_up(p, o, w_up_pool_b[l], w_up_attn_b[l], gates)
        xf, xb = _mix_out(m, w_o_b[l], xf, ln_mix_g[l], ln_mix_b[l], alpha)
        i = l // 2
        if l % 2 == 0:
            f = _swiglu(xb, ffn_w_gate, ffn_w_up, ffn_w_down, dense_nvalid, dense_eid, i, chunk, 512)
            xf, xb = _add_ln(xf, f, ln_ffn_g[l], ln_ffn_b[l], alpha)
        else:
            logits = _router(xf, moe_router[i])
            top_w, dest, row_tok, eid, nvalid, blk_valid, n_rows = _route(logits, chunk)
            xg = _gather_rows(xf, row_tok, blk_valid, n_rows)
            y = _swiglu(xg, moe_gate, moe_up, moe_down, nvalid, eid, i * n_exp, chunk, 512)
            xf, xb = _combine_ln(y, dest, top_w, xf, ln_ffn_g[l], ln_ffn_b[l], alpha)
    return xf.reshape(B, S, D)
```

```python
import functools

import jax
import jax.numpy as jnp
import numpy as np
from jax import lax
from jax.experimental import pallas as pl
from jax.experimental.pallas import tpu as pltpu

F32 = jnp.float32
BF16 = jnp.bfloat16

HEAD_DIM = 128
MOBA_BLOCK = 256
MOBA_TOPK = 3
MOBA_GROUP = 2
MOBA_HEADS_PER_STEP = 4
POOL_WINDOWS = (2, 4, 8, 16)
POOL_HALO = 16
MOE_TOPK = 2
LN_EPS = 1e-5
LOG2_E = float(np.log2(np.e))
LANES = 128
VMEM_LIMIT = 56 * 1024 * 1024

EXPERT_CHUNK = 1024
EXPERT_SUB = 512
EXPERT_TAIL = 256
GATHER_ROWS = 256
COMBINE_ROWS = 512
DMA_ISSUE_UNROLL = 8


def _tile(n, pref, mult=LANES):
    if n <= pref:
        return n
    t = (pref // mult) * mult
    while t > mult and n % t:
        t -= mult
    assert n % t == 0, (n, pref)
    return t


def _params(*sem):
    return pltpu.CompilerParams(dimension_semantics=sem, vmem_limit_bytes=VMEM_LIMIT)


def _layer_norm(v, g, b):
    mu = jnp.mean(v, axis=-1, keepdims=True)
    d = v - mu
    var = jnp.mean(d * d, axis=-1, keepdims=True)
    return d * lax.rsqrt(var + LN_EPS) * g + b


def _sigmoid(v):
    return 1.0 / (1.0 + jnp.exp(-v))


def _dot(a, b):
    return jnp.dot(a, b, preferred_element_type=F32)


def _dot_nt(a, b):
    return lax.dot_general(a, b, (((1,), (1,)), ((), ())), preferred_element_type=F32)


def _ln_kernel(x_ref, g_ref, b_ref, xf_ref, xb_ref):
    y = _layer_norm(x_ref[...], g_ref[...], b_ref[...])
    xf_ref[...] = y
    xb_ref[...] = y.astype(BF16)


def _ln_in(x, g, b):
    S, D = x.shape
    tm = _tile(S, 512, 8)
    row = pl.BlockSpec((tm, D), lambda i: (i, 0))
    vec = pl.BlockSpec((1, D), lambda i: (0, 0))
    return pl.pallas_call(
        _ln_kernel, name="ln_in",
        grid=(S // tm,),
        in_specs=[row, vec, vec],
        out_specs=[row, row],
        out_shape=[jax.ShapeDtypeStruct((S, D), F32), jax.ShapeDtypeStruct((S, D), BF16)],
        compiler_params=_params("parallel"),
    )(x, g.reshape(1, D), b.reshape(1, D))


def _add_ln_kernel(x_ref, f_ref, g_ref, b_ref, xf_ref, xb_ref, *, alpha):
    y = _layer_norm(alpha * x_ref[...] + f_ref[...], g_ref[...], b_ref[...])
    xf_ref[...] = y
    xb_ref[...] = y.astype(BF16)


def _add_ln(x, f, g, b, alpha):
    S, D = x.shape
    tm = _tile(S, 512, 8)
    row = pl.BlockSpec((tm, D), lambda i: (i, 0))
    vec = pl.BlockSpec((1, D), lambda i: (0, 0))
    return pl.pallas_call(
        functools.partial(_add_ln_kernel, alpha=alpha), name="add_ln",
        grid=(S // tm,),
        in_specs=[row, row, vec, vec],
        out_specs=[row, row],
        out_shape=[jax.ShapeDtypeStruct((S, D), F32), jax.ShapeDtypeStruct((S, D), BF16)],
        compiler_params=_params("parallel"),
    )(x, f, g.reshape(1, D), b.reshape(1, D))


def _proj_kernel(a_ref, b_ref, o_ref, bb_ref, *, mode, q_tiles, q_scale):
    @pl.when(pl.program_id(1) == 0)
    def _():
        bb_ref[...] = b_ref[0].astype(BF16)

    acc = _dot(a_ref[...], bb_ref[...])
    if mode == "qkv":
        acc = acc * jnp.where(pl.program_id(0) < q_tiles, q_scale, 1.0).astype(F32)
    elif mode == "sigmoid":
        acc = _sigmoid(acc)
    o_ref[...] = acc.astype(o_ref.dtype)


def _proj(xb, w_in, layer, col0, width, out_dtype, mode, q_width=0):
    S, D = xb.shape
    tm = _tile(S, 1024, 8)
    tn = _tile(int(np.gcd.reduce([width, col0 or width, q_width or width])), 1024)
    assert col0 % tn == 0 and width % tn == 0 and q_width % tn == 0
    off = col0 // tn
    kern = functools.partial(_proj_kernel, mode=mode, q_tiles=q_width // tn,
                             q_scale=HEAD_DIM ** -0.5 * LOG2_E)
    return pl.pallas_call(
        kern, name="proj_" + mode,
        grid=(width // tn, S // tm),
        in_specs=[pl.BlockSpec((tm, D), lambda n, i: (i, 0)),
                  pl.BlockSpec((1, D, tn), lambda n, i: (layer, 0, off + n))],
        out_specs=pl.BlockSpec((tm, tn), lambda n, i: (i, n)),
        out_shape=jax.ShapeDtypeStruct((S, width), out_dtype),
        scratch_shapes=[pltpu.VMEM((D, tn), BF16)],
        compiler_params=_params("parallel", "arbitrary"),
    )(xb, w_in)


def _pool_kernel(u_ref, uprev_ref, pw_ref, sc_ref, o_ref, ext_ref, *, tp, group):
    i = pl.program_id(0)
    halo = uprev_ref[...]
    ext_ref[0:POOL_HALO, :] = jnp.where(i > 0, halo, jnp.zeros_like(halo))
    ext_ref[POOL_HALO:, :] = u_ref[...]
    t = i * tp + lax.broadcasted_iota(jnp.int32, (tp, 1), 0)
    for g, w in enumerate(POOL_WINDOWS):
        cols = slice(g * group, (g + 1) * group)
        s = ext_ref[:, cols]
        k = 1
        while k < w:
            s = s + pltpu.roll(s, k, 0)
            k *= 2
        cnt = jnp.minimum(t + 1, w).astype(F32)
        cur = u_ref[:, cols]
        pooled = s[POOL_HALO:, :] / cnt - cur
        y = _dot(pooled.astype(BF16), pw_ref[g]) * sc_ref[:, cols]
        o_ref[:, cols] = y.astype(o_ref.dtype)


def _pool(u, pool_w_b, pool_scale):
    S, PW = u.shape
    G, C, _ = pool_w_b.shape
    assert G == len(POOL_WINDOWS) and G * C == PW
    tp = _tile(S, 512, POOL_HALO)
    per = tp // POOL_HALO
    return pl.pallas_call(
        functools.partial(_pool_kernel, tp=tp, group=C), name="pool",
        grid=(S // tp,),
        in_specs=[pl.BlockSpec((tp, PW), lambda i: (i, 0)),
                  pl.BlockSpec((POOL_HALO, PW), lambda i: (jnp.maximum(i * per - 1, 0), 0)),
                  pl.BlockSpec((G, C, C), lambda i: (0, 0, 0)),
                  pl.BlockSpec((1, PW), lambda i: (0, 0))],
        out_specs=pl.BlockSpec((tp, PW), lambda i: (i, 0)),
        out_shape=jax.ShapeDtypeStruct((S, PW), BF16),
        scratch_shapes=[pltpu.VMEM((tp + POOL_HALO, PW), F32)],
        compiler_params=_params("parallel"),
    )(u, u, pool_w_b, pool_scale.reshape(1, PW))


def _moba_kernel(q_ref, k_ref, v_ref, o_ref, kmean_ref, vt_ref, bias_ref, sa_ref, sb_ref,
                 *, nb, group, heads):
    i = pl.program_id(1)
    B = MOBA_BLOCK
    neg = jnp.float32(-jnp.inf)
    U = group
    last = nb // U - 1
    lanes = [slice(h * HEAD_DIM, (h + 1) * HEAD_DIM) for h in range(heads)]

    @pl.when(i == 0)
    def _():
        def prep(g, carry):
            for h in range(heads):
                for u in range(U):
                    rj = pl.multiple_of((g * U + u) * B, B)
                    vt_ref[h, g, :, u * B:(u + 1) * B] = (
                        v_ref[pl.ds(rj, B), lanes[h]].astype(F32).T.astype(BF16))
                    kj = k_ref[pl.ds(rj, B), lanes[h]].astype(F32)
                    kmean_ref[h, pl.ds(g * U + u, 1), :] = jnp.sum(kj, axis=0, keepdims=True) / B
            return carry

        lax.fori_loop(0, nb // U, prep, 0)

    row0 = pl.multiple_of(i * B, B)
    blk = lax.broadcasted_iota(jnp.int32, (nb, B), 0)
    key_id = lax.broadcasted_iota(jnp.int32, (B, B), 0)
    qry_id = lax.broadcasted_iota(jnp.int32, (B, B), 1)
    qts = [q_ref[:, lanes[h]].astype(F32).T.astype(BF16) for h in range(heads)]

    def scores(h, g):
        r0 = pl.multiple_of(jnp.minimum(g, last) * (U * B), U * B)
        return _dot(k_ref[pl.ds(r0, U * B), lanes[h]], qts[h])

    gates, s_own, vt_own = [], [], []
    for h in range(heads):
        km = kmean_ref[h]
        km_hi = km.astype(BF16)
        km_lo = (km - km_hi.astype(F32)).astype(BF16)
        gates.append(_dot(km_hi, qts[h]) + _dot(km_lo, qts[h]))
        s_own.append(_dot(k_ref[pl.ds(row0, B), lanes[h]], qts[h]))
        vt_own.append(v_ref[pl.ds(row0, B), lanes[h]].astype(F32).T.astype(BF16))
    for h in range(heads):
        sa_ref[h] = scores(h, 0)

    init = []
    for h in range(heads):
        g = jnp.where(blk < i, gates[h], neg)
        bias = jnp.full((nb, B), neg, F32)
        for _ in range(MOBA_TOPK):
            m = jnp.max(g, axis=0, keepdims=True)
            idx = jnp.min(jnp.where(g == m, blk, nb), axis=0, keepdims=True)
            hit = blk == idx
            bias = jnp.where(hit & (m > neg), 0.0, bias)
            g = jnp.where(hit, neg, g)
        bias_ref[h, 0:nb, :] = bias
        bias_ref[h, nb:, :] = jnp.full((U, B), neg, F32)

        s = jnp.where(key_id <= qry_id, s_own[h], neg)
        m0 = jnp.max(s, axis=0, keepdims=True)
        p = jnp.exp2(s - m0)
        l0 = jnp.sum(p, axis=0, keepdims=True)
        init.append((m0, l0, p.astype(BF16)))
    init = [(m0, l0, _dot(vt_own[h], p)) for h, (m0, l0, p) in enumerate(init)]

    def process(h, s_ref, g, state):
        m_run, l_run, acc = state
        sel_u = [bias_ref[h, pl.ds(g * U + u, 1), :] for u in range(U)]
        s_u = [s_ref[h, u * B:(u + 1) * B, :] for u in range(U)]
        m_new = m_run
        for u in range(U):
            m_new = jnp.maximum(m_new, jnp.max(s_u[u], axis=0, keepdims=True) + sel_u[u])
        a = jnp.exp2(m_run - m_new)
        l_new = a * l_run
        acc_new = a * acc
        gv = jnp.minimum(g, last)
        for u in range(U):
            p = jnp.exp2(s_u[u] - (m_new - sel_u[u]))
            l_new = l_new + jnp.sum(p, axis=0, keepdims=True)
            acc_new = acc_new + _dot(vt_ref[h, gv, :, u * B:(u + 1) * B], p.astype(BF16))
        return m_new, l_new, acc_new

    def pair(t, states):
        for h in range(heads):
            sb_ref[h] = scores(h, 2 * t + 1)
        states = tuple(process(h, sa_ref, 2 * t, states[h]) for h in range(heads))
        for h in range(heads):
            sa_ref[h] = scores(h, 2 * t + 2)
        return tuple(process(h, sb_ref, 2 * t + 1, states[h]) for h in range(heads))

    n_groups = (i + U - 1) // U
    final = lax.fori_loop(0, (n_groups + 1) // 2, pair, tuple(init))
    for h in range(heads):
        _, l_fin, acc_fin = final[h]
        o_ref[:, lanes[h]] = (acc_fin / l_fin).T.astype(o_ref.dtype)


def _moba(qkv, n_heads):
    S = qkv.shape[0]
    assert S % MOBA_BLOCK == 0
    nb = S // MOBA_BLOCK
    H = n_heads
    group = MOBA_GROUP
    hp = MOBA_HEADS_PER_STEP
    assert nb % group == 0 and H % hp == 0
    G = H // hp
    W = hp * HEAD_DIM
    return pl.pallas_call(
        functools.partial(_moba_kernel, nb=nb, group=group, heads=hp), name="moba",
        grid=(G, nb),
        in_specs=[pl.BlockSpec((MOBA_BLOCK, W), lambda h, i: (i, h)),
                  pl.BlockSpec((S, W), lambda h, i: (0, G + h)),
                  pl.BlockSpec((S, W), lambda h, i: (0, 2 * G + h))],
        out_specs=pl.BlockSpec((MOBA_BLOCK, W), lambda h, i: (i, h)),
        out_shape=jax.ShapeDtypeStruct((S, H * HEAD_DIM), BF16),
        scratch_shapes=[pltpu.VMEM((hp, nb, HEAD_DIM), F32),
                        pltpu.VMEM((hp, nb // group, HEAD_DIM, group * MOBA_BLOCK), BF16),
                        pltpu.VMEM((hp, nb + group, MOBA_BLOCK), F32),
                        pltpu.VMEM((hp, group * MOBA_BLOCK, MOBA_BLOCK), F32),
                        pltpu.VMEM((hp, group * MOBA_BLOCK, MOBA_BLOCK), F32)],
        compiler_params=_params("parallel", "arbitrary"),
    )(qkv, qkv, qkv)


def _mix_up_kernel(p_ref, a_ref, wp_ref, wa_ref, gp_ref, ga_ref, o_ref):
    y_pool = _dot(p_ref[...], wp_ref[...])
    y_attn = _dot(a_ref[...], wa_ref[...])
    o_ref[...] = (gp_ref[...] * y_pool + ga_ref[...] * y_attn).astype(o_ref.dtype)


def _mix_up(p, o, w_up_pool_b, w_up_attn_b, gates):
    S, PW = p.shape
    AW = o.shape[1]
    D = w_up_pool_b.shape[1]
    tm = _tile(S, 1024, 8)
    tn = _tile(D, 1024)
    nt = D // tn
    return pl.pallas_call(
        _mix_up_kernel, name="mix_up",
        grid=(S // tm, nt),
        in_specs=[pl.BlockSpec((tm, PW), lambda i, n: (i, 0)),
                  pl.BlockSpec((tm, AW), lambda i, n: (i, 0)),
                  pl.BlockSpec((PW, tn), lambda i, n: (0, n)),
                  pl.BlockSpec((AW, tn), lambda i, n: (0, n)),
                  pl.BlockSpec((tm, tn), lambda i, n: (i, n)),
                  pl.BlockSpec((tm, tn), lambda i, n: (i, nt + n))],
        out_specs=pl.BlockSpec((tm, tn), lambda i, n: (i, n)),
        out_shape=jax.ShapeDtypeStruct((S, D), BF16),
        compiler_params=_params("parallel", "arbitrary"),
    )(p, o, w_up_pool_b, w_up_attn_b, gates, gates)


def _mix_out_kernel(m_ref, w_ref, x_ref, g_ref, b_ref, xf_ref, xb_ref, *, alpha):
    mix = _dot(m_ref[...], w_ref[...])
    y = _layer_norm(alpha * x_ref[...] + mix, g_ref[...], b_ref[...])
    xf_ref[...] = y
    xb_ref[...] = y.astype(BF16)


def _mix_out(m, w_o_b, x, g, b, alpha):
    S, D = x.shape
    tm = _tile(S, 512, 8)
    row = lambda dt: pl.BlockSpec((tm, D), lambda i: (i, 0))
    vec = pl.BlockSpec((1, D), lambda i: (0, 0))
    return pl.pallas_call(
        functools.partial(_mix_out_kernel, alpha=alpha), name="mix_out",
        grid=(S // tm,),
        in_specs=[row(BF16), pl.BlockSpec((D, D), lambda i: (0, 0)), row(F32), vec, vec],
        out_specs=[row(F32), row(BF16)],
        out_shape=[jax.ShapeDtypeStruct((S, D), F32), jax.ShapeDtypeStruct((S, D), BF16)],
        compiler_params=_params("parallel"),
    )(m, w_o_b, x, g.reshape(1, D), b.reshape(1, D))


def _swiglu_kernel(nvalid_ref, eid_ref, x_ref, wg_ref, wu_ref, wd_ref, o_ref, *, sub, tail):
    c = pl.program_id(0)
    f = pl.program_id(1)
    nv = nvalid_ref[c]
    rows = x_ref.shape[0]

    @pl.when(f == 0)
    def _():
        o_ref[...] = jnp.zeros(o_ref.shape, F32)

    def run(n_rows):
        wg = wg_ref[0].astype(BF16)
        wu = wu_ref[0].astype(BF16)
        wd = wd_ref[0].astype(BF16)
        for s0 in range(0, n_rows, sub):
            rs = slice(s0, min(s0 + sub, n_rows))
            xs = x_ref[rs, :]
            gate = _dot(xs, wg)
            up = _dot(xs, wu)
            h = (gate * _sigmoid(gate) * up).astype(BF16)
            o_ref[rs, :] += _dot(h, wd)

    n_var = rows // tail
    for k in range(1, n_var + 1):
        lo = (k - 1) * tail
        cond = (nv > lo) if k == n_var else ((nv > lo) & (nv <= lo + tail))
        pl.when(cond)(functools.partial(run, k * tail))


def _swiglu(xrows, w_gate, w_up, w_down, nvalid, eid, w_base, chunk, tf_pref):
    R, D = xrows.shape
    F = w_gate.shape[-1]
    tf = _tile(F, tf_pref)
    sub = min(EXPERT_SUB, chunk)
    tail = min(EXPERT_TAIL, sub)
    assert R % chunk == 0 and chunk % sub == 0 and sub % tail == 0 and tail % GATHER_ROWS == 0

    def w_in_map(c, f, nv, e):
        return (w_base + e[c], 0, jnp.where(nv[c] > 0, f, 0))

    def w_out_map(c, f, nv, e):
        return (w_base + e[c], jnp.where(nv[c] > 0, f, 0), 0)

    grid_spec = pltpu.PrefetchScalarGridSpec(
        num_scalar_prefetch=2,
        grid=(R // chunk, F // tf),
        in_specs=[pl.BlockSpec((chunk, D), lambda c, f, nv, e: (c, 0)),
                  pl.BlockSpec((1, D, tf), w_in_map),
                  pl.BlockSpec((1, D, tf), w_in_map),
                  pl.BlockSpec((1, tf, D), w_out_map)],
        out_specs=pl.BlockSpec((chunk, D), lambda c, f, nv, e: (c, 0)),
    )
    return pl.pallas_call(
        functools.partial(_swiglu_kernel, sub=sub, tail=tail), name="swiglu",
        grid_spec=grid_spec,
        out_shape=jax.ShapeDtypeStruct((R, D), F32),
        compiler_params=_params("parallel", "arbitrary"),
    )(nvalid, eid, xrows, w_gate, w_up, w_down)


def _router_kernel(x_ref, rh_ref, rl_ref, o_ref):
    x = x_ref[...]
    xh = x.astype(BF16)
    xl = (x - xh.astype(F32)).astype(BF16)
    o_ref[...] = _dot(xh, rh_ref[...]) + _dot(xl, rh_ref[...]) + _dot(xh, rl_ref[...])


def _router(x, router_w):
    S, D = x.shape
    E = router_w.shape[1]
    wpad = jnp.pad(router_w, ((0, 0), (0, LANES - E)))
    rh = wpad.astype(BF16)
    rl = (wpad - rh.astype(F32)).astype(BF16)
    tm = _tile(S, 512, 8)
    wspec = pl.BlockSpec((D, LANES), lambda i: (0, 0))
    logits = pl.pallas_call(
        _router_kernel, name="router",
        grid=(S // tm,),
        in_specs=[pl.BlockSpec((tm, D), lambda i: (i, 0)), wspec, wspec],
        out_specs=pl.BlockSpec((tm, LANES), lambda i: (i, 0)),
        out_shape=jax.ShapeDtypeStruct((S, LANES), F32),
        compiler_params=_params("parallel"),
    )(x, rh, rl)
    return logits[:, :E]


def _gather_kernel(tok_ref, valid_ref, x_hbm, o_ref, buf_ref, sem, *, rows, n_blocks):
    b = pl.program_id(0)

    def start(blk, slot):
        base = blk * rows

        def issue(r, carry):
            tok = tok_ref[base + r]
            pltpu.make_async_copy(x_hbm.at[pl.ds(tok, 1), :], buf_ref.at[slot, pl.ds(r, 1), :],
                                  sem.at[slot]).start()
            return carry

        lax.fori_loop(0, rows, issue, 0, unroll=DMA_ISSUE_UNROLL)

    @pl.when((b == 0) & (valid_ref[0] > 0))
    def _():
        start(0, 0)

    nxt = jnp.minimum(b + 1, n_blocks - 1)

    @pl.when((b + 1 < n_blocks) & (valid_ref[nxt] > 0))
    def _():
        start(nxt, nxt % 2)

    @pl.when(valid_ref[b] > 0)
    def _():
        slot = b % 2
        pltpu.make_async_copy(x_hbm.at[pl.ds(0, rows), :], buf_ref.at[slot], sem.at[slot]).wait()
        o_ref[...] = buf_ref[slot].astype(o_ref.dtype)

    @pl.when(valid_ref[b] == 0)
    def _():
        o_ref[...] = jnp.zeros(o_ref.shape, o_ref.dtype)


def _gather_rows(x, row_tok, blk_valid, n_rows):
    S, D = x.shape
    rows = GATHER_ROWS
    grid_spec = pltpu.PrefetchScalarGridSpec(
        num_scalar_prefetch=2,
        grid=(n_rows // rows,),
        in_specs=[pl.BlockSpec(memory_space=pl.ANY)],
        out_specs=pl.BlockSpec((rows, D), lambda b, t, v: (b, 0)),
        scratch_shapes=[pltpu.VMEM((2, rows, D), F32), pltpu.SemaphoreType.DMA((2,))],
    )
    return pl.pallas_call(
        functools.partial(_gather_kernel, rows=rows, n_blocks=n_rows // rows), name="gather_rows",
        grid_spec=grid_spec,
        out_shape=jax.ShapeDtypeStruct((n_rows, D), BF16),
        compiler_params=_params("arbitrary"),
    )(row_tok, blk_valid, x)


def _combine_kernel(pos_ref, y_hbm, w_ref, x_ref, g_ref, b_ref, xf_ref, xb_ref, buf_ref, sem,
                    *, rows, n_blocks, alpha):
    b = pl.program_id(0)

    def start(blk, slot):
        base = blk * rows * MOE_TOPK

        def issue(r, carry):
            for k in range(MOE_TOPK):
                src = pos_ref[base + r * MOE_TOPK + k]
                pltpu.make_async_copy(y_hbm.at[pl.ds(src, 1), :], buf_ref.at[slot, k, pl.ds(r, 1), :],
                                      sem.at[slot]).start()
            return carry

        lax.fori_loop(0, rows, issue, 0, unroll=DMA_ISSUE_UNROLL)

    @pl.when(b == 0)
    def _():
        start(0, 0)

    @pl.when(b + 1 < n_blocks)
    def _():
        start(b + 1, (b + 1) % 2)

    slot = b % 2
    for k in range(MOE_TOPK):
        pltpu.make_async_copy(y_hbm.at[pl.ds(0, rows), :], buf_ref.at[slot, k], sem.at[slot]).wait()
    w = w_ref[...]
    f = w[:, 0:1] * buf_ref[slot, 0] + w[:, 1:2] * buf_ref[slot, 1]
    y = _layer_norm(alpha * x_ref[...] + f, g_ref[...], b_ref[...])
    xf_ref[...] = y
    xb_ref[...] = y.astype(BF16)


def _combine_ln(y, pos, top_w, x, g, b, alpha):
    S, D = x.shape
    rows = _tile(S, COMBINE_ROWS, 8)
    row = pl.BlockSpec((rows, D), lambda i, p: (i, 0))
    vec = pl.BlockSpec((1, D), lambda i, p: (0, 0))
    grid_spec = pltpu.PrefetchScalarGridSpec(
        num_scalar_prefetch=1,
        grid=(S // rows,),
        in_specs=[pl.BlockSpec(memory_space=pl.ANY),
                  pl.BlockSpec((rows, MOE_TOPK), lambda i, p: (i, 0)),
                  row, vec, vec],
        out_specs=[row, row],
        scratch_shapes=[pltpu.VMEM((2, MOE_TOPK, rows, D), F32), pltpu.SemaphoreType.DMA((2,))],
    )
    return pl.pallas_call(
        functools.partial(_combine_kernel, rows=rows, n_blocks=S // rows, alpha=alpha), name="combine_ln",
        grid_spec=grid_spec,
        out_shape=[jax.ShapeDtypeStruct((S, D), F32), jax.ShapeDtypeStruct((S, D), BF16)],
        compiler_params=_params("arbitrary"),
    )(pos, y, top_w, x, g.reshape(1, D), b.reshape(1, D))


def _route(logits, chunk):
    S, E = logits.shape
    top_logit, top_e = lax.top_k(logits, MOE_TOPK)
    top_w = jax.nn.softmax(top_logit, axis=-1)
    n_assign = S * MOE_TOPK
    flat_e = top_e.reshape(-1).astype(jnp.int32)
    onehot = (flat_e[:, None] == jnp.arange(E, dtype=jnp.int32)[None, :]).astype(jnp.int32)
    running = jnp.cumsum(onehot, axis=0)
    rank = jnp.sum(running * onehot, axis=1) - 1
    counts = running[-1]
    padded = (counts + chunk - 1) // chunk * chunk
    ends = jnp.cumsum(padded)
    pstart = ends - padded
    dest = pstart[flat_e] + rank
    n_rows = (-(-n_assign // chunk) + E) * chunk
    n_chunks = n_rows // chunk
    flat_tok = jnp.arange(n_assign, dtype=jnp.int32) // MOE_TOPK
    row_tok = jnp.zeros((n_rows,), jnp.int32).at[dest].set(flat_tok)
    chunk_start = jnp.arange(n_chunks, dtype=jnp.int32) * chunk
    eid = jnp.minimum(jnp.searchsorted(ends, chunk_start, side="right"), E - 1).astype(jnp.int32)
    nvalid = jnp.clip(counts[eid] - (chunk_start - pstart[eid]), 0, chunk).astype(jnp.int32)
    per = chunk // GATHER_ROWS
    blk = jnp.arange(n_rows // GATHER_ROWS, dtype=jnp.int32)
    blk_valid = (nvalid[blk // per] > (blk % per) * GATHER_ROWS).astype(jnp.int32)
    return top_w, dest.astype(jnp.int32), row_tok, eid, nvalid, blk_valid, n_rows


def kernel(x, ln_in_g, ln_in_b, w_in, pool_w, pool_scale, w_up_pool, w_up_attn, w_o, ln_mix_g, ln_mix_b, ffn_w_gate, ffn_w_up, ffn_w_down, moe_router, moe_w_gate, moe_w_up, moe_w_down, ln_ffn_g, ln_ffn_b):
    B, S, D = x.shape
    assert B == 1
    depth = w_in.shape[0]
    PW = w_up_pool.shape[1]
    AW = w_up_attn.shape[1]
    n_heads = AW // HEAD_DIM
    n_exp = moe_router.shape[-1]
    alpha = float((2 * depth) ** 0.25)

    pool_w_b = pool_w.astype(BF16)
    w_up_pool_b = w_up_pool.astype(BF16)
    w_up_attn_b = w_up_attn.astype(BF16)
    w_o_b = w_o.astype(BF16)
    moe_gate = moe_w_gate.reshape((-1,) + moe_w_gate.shape[2:])
    moe_up = moe_w_up.reshape((-1,) + moe_w_up.shape[2:])
    moe_down = moe_w_down.reshape((-1,) + moe_w_down.shape[2:])

    chunk = min(EXPERT_CHUNK, S)
    dense_nvalid = jnp.full((S // chunk,), chunk, jnp.int32)
    dense_eid = jnp.zeros((S // chunk,), jnp.int32)

    xf, xb = _ln_in(x.reshape(S, D), ln_in_g, ln_in_b)
    for l in range(depth):
        u = _proj(xb, w_in, l, 0, PW, F32, "plain")
        qkv = _proj(xb, w_in, l, PW, 3 * AW, BF16, "qkv", q_width=AW)
        gates = _proj(xb, w_in, l, PW + 3 * AW, 2 * D, BF16, "sigmoid")
        p = _pool(u, pool_w_b[l], pool_scale[l])
        o = _moba(qkv, n_heads)
        m = _mix_up(p, o, w_up_pool_b[l], w_up_attn_b[l], gates)
        xf, xb = _mix_out(m, w_o_b[l], xf, ln_mix_g[l], ln_mix_b[l], alpha)
        i = l // 2
        if l % 2 == 0:
            f = _swiglu(xb, ffn_w_gate, ffn_w_up, ffn_w_down, dense_nvalid, dense_eid, i, chunk, 512)
            xf, xb = _add_ln(xf, f, ln_ffn_g[l], ln_ffn_b[l], alpha)
        else:
            logits = _router(xf, moe_router[i])
            top_w, dest, row_tok, eid, nvalid, blk_valid, n_rows = _route(logits, chunk)
            xg = _gather_rows(xf, row_tok, blk_valid, n_rows)
            y = _swiglu(xg, moe_gate, moe_up, moe_down, nvalid, eid, i * n_exp, chunk, 512)
            xf, xb = _combine_ln(y, dest, top_w, xf, ln_ffn_g[l], ln_ffn_b[l], alpha)
    return xf.reshape(B, S, D)
```

```python
import functools

import jax
import jax.numpy as jnp
import numpy as np
from jax import lax
from jax.experimental import pallas as pl
from jax.experimental.pallas import tpu as pltpu

F32 = jnp.float32
BF16 = jnp.bfloat16

HEAD_DIM = 128
MOBA_BLOCK = 256
MOBA_TOPK = 3
MOBA_GROUP = 2
MOBA_HEADS_PER_STEP = 4
POOL_WINDOWS = (2, 4, 8, 16)
POOL_HALO = 16
MOE_TOPK = 2
LN_EPS = 1e-5
LOG2_E = float(np.log2(np.e))
LANES = 128
VMEM_LIMIT = 56 * 1024 * 1024

EXPERT_CHUNK = 1024
EXPERT_SUB = 512
EXPERT_TAIL = 256
GATHER_ROWS = 256
COMBINE_ROWS = 512
DMA_ISSUE_UNROLL = 8


def _tile(n, pref, mult=LANES):
    if n <= pref:
        return n
    t = (pref // mult) * mult
    while t > mult and n % t:
        t -= mult
    assert n % t == 0, (n, pref)
    return t


def _params(*sem):
    return pltpu.CompilerParams(dimension_semantics=sem, vmem_limit_bytes=VMEM_LIMIT)


def _layer_norm(v, g, b):
    mu = jnp.mean(v, axis=-1, keepdims=True)
    d = v - mu
    var = jnp.mean(d * d, axis=-1, keepdims=True)
    return d * lax.rsqrt(var + LN_EPS) * g + b


def _sigmoid(v):
    return 1.0 / (1.0 + jnp.exp(-v))


def _dot(a, b):
    return jnp.dot(a, b, preferred_element_type=F32)


def _dot_nt(a, b):
    return lax.dot_general(a, b, (((1,), (1,)), ((), ())), preferred_element_type=F32)


def _ln_kernel(x_ref, g_ref, b_ref, xf_ref, xb_ref):
    y = _layer_norm(x_ref[...], g_ref[...], b_ref[...])
    xf_ref[...] = y
    xb_ref[...] = y.astype(BF16)


def _ln_in(x, g, b):
    S, D = x.shape
    tm = _tile(S, 512, 8)
    row = pl.BlockSpec((tm, D), lambda i: (i, 0))
    vec = pl.BlockSpec((1, D), lambda i: (0, 0))
    return pl.pallas_call(
        _ln_kernel, name="ln_in",
        grid=(S // tm,),
        in_specs=[row, vec, vec],
        out_specs=[row, row],
        out_shape=[jax.ShapeDtypeStruct((S, D), F32), jax.ShapeDtypeStruct((S, D), BF16)],
        compiler_params=_params("parallel"),
    )(x, g.reshape(1, D), b.reshape(1, D))


def _add_ln_kernel(x_ref, f_ref, g_ref, b_ref, xf_ref, xb_ref, *, alpha):
    y = _layer_norm(alpha * x_ref[...] + f_ref[...], g_ref[...], b_ref[...])
    xf_ref[...] = y
    xb_ref[...] = y.astype(BF16)


def _add_ln(x, f, g, b, alpha):
    S, D = x.shape
    tm = _tile(S, 512, 8)
    row = pl.BlockSpec((tm, D), lambda i: (i, 0))
    vec = pl.BlockSpec((1, D), lambda i: (0, 0))
    return pl.pallas_call(
        functools.partial(_add_ln_kernel, alpha=alpha), name="add_ln",
        grid=(S // tm,),
        in_specs=[row, row, vec, vec],
        out_specs=[row, row],
        out_shape=[jax.ShapeDtypeStruct((S, D), F32), jax.ShapeDtypeStruct((S, D), BF16)],
        compiler_params=_params("parallel"),
    )(x, f, g.reshape(1, D), b.reshape(1, D))


def _proj_kernel(a_ref, b_ref, o_ref, bb_ref, *, mode, q_tiles, q_scale):
    @pl.when(pl.program_id(1) == 0)
    def _():
        bb_ref[...] = b_ref[0].astype(BF16)

    acc = _dot(a_ref[...], bb_ref[...])
    if mode == "qkv":
        acc = acc * jnp.where(pl.program_id(0) < q_tiles, q_scale, 1.0).astype(F32)
    elif mode == "sigmoid":
        acc = _sigmoid(acc)
    o_ref[...] = acc.astype(o_ref.dtype)


def _proj(xb, w_in, layer, col0, width, out_dtype, mode, q_width=0):
    S, D = xb.shape
    tm = _tile(S, 1024, 8)
    tn = _tile(int(np.gcd.reduce([width, col0 or width, q_width or width])), 1024)
    assert col0 % tn == 0 and width % tn == 0 and q_width % tn == 0
    off = col0 // tn
    kern = functools.partial(_proj_kernel, mode=mode, q_tiles=q_width // tn,
                             q_scale=HEAD_DIM ** -0.5 * LOG2_E)
    return pl.pallas_call(
        kern, name="proj_" + mode,
        grid=(width // tn, S // tm),
        in_specs=[pl.BlockSpec((tm, D), lambda n, i: (i, 0)),
                  pl.BlockSpec((1, D, tn), lambda n, i: (layer, 0, off + n))],
        out_specs=pl.BlockSpec((tm, tn), lambda n, i: (i, n)),
        out_shape=jax.ShapeDtypeStruct((S, width), out_dtype),
        scratch_shapes=[pltpu.VMEM((D, tn), BF16)],
        compiler_params=_params("parallel", "arbitrary"),
    )(xb, w_in)


def _pool_kernel(u_ref, uprev_ref, pw_ref, sc_ref, o_ref, ext_ref, *, tp, group):
    i = pl.program_id(0)
    halo = uprev_ref[...]
    ext_ref[0:POOL_HALO, :] = jnp.where(i > 0, halo, jnp.zeros_like(halo))
    ext_ref[POOL_HALO:, :] = u_ref[...]
    t = i * tp + lax.broadcasted_iota(jnp.int32, (tp, 1), 0)
    for g, w in enumerate(POOL_WINDOWS):
        cols = slice(g * group, (g + 1) * group)
        s = ext_ref[:, cols]
        k = 1
        while k < w:
            s = s + pltpu.roll(s, k, 0)
            k *= 2
        cnt = jnp.minimum(t + 1, w).astype(F32)
        cur = u_ref[:, cols]
        pooled = s[POOL_HALO:, :] / cnt - cur
        y = _dot(pooled.astype(BF16), pw_ref[g]) * sc_ref[:, cols]
        o_ref[:, cols] = y.astype(o_ref.dtype)


def _pool(u, pool_w_b, pool_scale):
    S, PW = u.shape
    G, C, _ = pool_w_b.shape
    assert G == len(POOL_WINDOWS) and G * C == PW
    tp = _tile(S, 512, POOL_HALO)
    per = tp // POOL_HALO
    return pl.pallas_call(
        functools.partial(_pool_kernel, tp=tp, group=C), name="pool",
        grid=(S // tp,),
        in_specs=[pl.BlockSpec((tp, PW), lambda i: (i, 0)),
                  pl.BlockSpec((POOL_HALO, PW), lambda i: (jnp.maximum(i * per - 1, 0), 0)),
                  pl.BlockSpec((G, C, C), lambda i: (0, 0, 0)),
                  pl.BlockSpec((1, PW), lambda i: (0, 0))],
        out_specs=pl.BlockSpec((tp, PW), lambda i: (i, 0)),
        out_shape=jax.ShapeDtypeStruct((S, PW), BF16),
        scratch_shapes=[pltpu.VMEM((tp + POOL_HALO, PW), F32)],
        compiler_params=_params("parallel"),
    )(u, u, pool_w_b, pool_scale.reshape(1, PW))


def _moba_kernel(q_ref, k_ref, v_ref, o_ref, kmean_ref, vt_ref, bias_ref, sa_ref, sb_ref,
                 *, nb, group, heads):
    i = pl.program_id(1)
    B = MOBA_BLOCK
    neg = jnp.float32(-jnp.inf)
    U = group
    last = nb // U - 1
    lanes = [slice(h * HEAD_DIM, (h + 1) * HEAD_DIM) for h in range(heads)]

    @pl.when(i == 0)
    def _():
        def prep(g, carry):
            for h in range(heads):
                for u in range(U):
                    rj = pl.multiple_of((g * U + u) * B, B)
                    vt_ref[h, g, :, u * B:(u + 1) * B] = (
                        v_ref[pl.ds(rj, B), lanes[h]].astype(F32).T.astype(BF16))
                    kj = k_ref[pl.ds(rj, B), lanes[h]].astype(F32)
                    kmean_ref[h, pl.ds(g * U + u, 1), :] = jnp.sum(kj, axis=0, keepdims=True) / B
            return carry

        lax.fori_loop(0, nb // U, prep, 0)

    row0 = pl.multiple_of(i * B, B)
    blk = lax.broadcasted_iota(jnp.int32, (nb, B), 0)
    key_id = lax.broadcasted_iota(jnp.int32, (B, B), 0)
    qry_id = lax.broadcasted_iota(jnp.int32, (B, B), 1)
    qts = [q_ref[:, lanes[h]].astype(F32).T.astype(BF16) for h in range(heads)]

    def scores(h, g):
        r0 = pl.multiple_of(jnp.minimum(g, last) * (U * B), U * B)
        return _dot(k_ref[pl.ds(r0, U * B), lanes[h]], qts[h])

    gates, s_own, vt_own = [], [], []
    for h in range(heads):
        km = kmean_ref[h]
        km_hi = km.astype(BF16)
        km_lo = (km - km_hi.astype(F32)).astype(BF16)
        gates.append(_dot(km_hi, qts[h]) + _dot(km_lo, qts[h]))
        s_own.append(_dot(k_ref[pl.ds(row0, B), lanes[h]], qts[h]))
        vt_own.append(v_ref[pl.ds(row0, B), lanes[h]].astype(F32).T.astype(BF16))
    for h in range(heads):
        sa_ref[h] = scores(h, 0)

    init = []
    for h in range(heads):
        g = jnp.where(blk < i, gates[h], neg)
        bias = jnp.full((nb, B), neg, F32)
        for _ in range(MOBA_TOPK):
            m = jnp.max(g, axis=0, keepdims=True)
            idx = jnp.min(jnp.where(g == m, blk, nb), axis=0, keepdims=True)
            hit = blk == idx
            bias = jnp.where(hit & (m > neg), 0.0, bias)
            g = jnp.where(hit, neg, g)
        bias_ref[h, 0:nb, :] = bias
        bias_ref[h, nb:, :] = jnp.full((U, B), neg, F32)

        s = jnp.where(key_id <= qry_id, s_own[h], neg)
        m0 = jnp.max(s, axis=0, keepdims=True)
        p = jnp.exp2(s - m0)
        l0 = jnp.sum(p, axis=0, keepdims=True)
        init.append((m0, l0, p.astype(BF16)))
    init = [(m0, l0, _dot(vt_own[h], p)) for h, (m0, l0, p) in enumerate(init)]

    def process(h, s_ref, g, state):
        m_run, l_run, acc = state
        sel_u = [bias_ref[h, pl.ds(g * U + u, 1), :] for u in range(U)]
        s_u = [s_ref[h, u * B:(u + 1) * B, :] for u in range(U)]
        m_new = m_run
        for u in range(U):
            m_new = jnp.maximum(m_new, jnp.max(s_u[u], axis=0, keepdims=True) + sel_u[u])
        a = jnp.exp2(m_run - m_new)
        l_new = a * l_run
        acc_new = a * acc
        gv = jnp.minimum(g, last)
        for u in range(U):
            p = jnp.exp2(s_u[u] - (m_new - sel_u[u]))
            l_new = l_new + jnp.sum(p, axis=0, keepdims=True)
            acc_new = acc_new + _dot(vt_ref[h, gv, :, u * B:(u + 1) * B], p.astype(BF16))
        return m_new, l_new, acc_new

    def pair(t, states):
        for h in range(heads):
            sb_ref[h] = scores(h, 2 * t + 1)
        states = tuple(process(h, sa_ref, 2 * t, states[h]) for h in range(heads))
        for h in range(heads):
            sa_ref[h] = scores(h, 2 * t + 2)
        return tuple(process(h, sb_ref, 2 * t + 1, states[h]) for h in range(heads))

    n_groups = (i + U - 1) // U
    final = lax.fori_loop(0, (n_groups + 1) // 2, pair, tuple(init))
    for h in range(heads):
        _, l_fin, acc_fin = final[h]
        o_ref[:, lanes[h]] = (acc_fin / l_fin).T.astype(o_ref.dtype)


def _moba(qkv, n_heads):
    S = qkv.shape[0]
    assert S % MOBA_BLOCK == 0
    nb = S // MOBA_BLOCK
    H = n_heads
    group = MOBA_GROUP
    hp = MOBA_HEADS_PER_STEP
    assert nb % group == 0 and H % hp == 0
    G = H // hp
    W = hp * HEAD_DIM
    return pl.pallas_call(
        functools.partial(_moba_kernel, nb=nb, group=group, heads=hp), name="moba",
        grid=(G, nb),
        in_specs=[pl.BlockSpec((MOBA_BLOCK, W), lambda h, i: (i, h)),
                  pl.BlockSpec((S, W), lambda h, i: (0, G + h)),
                  pl.BlockSpec((S, W), lambda h, i: (0, 2 * G + h))],
        out_specs=pl.BlockSpec((MOBA_BLOCK, W), lambda h, i: (i, h)),
        out_shape=jax.ShapeDtypeStruct((S, H * HEAD_DIM), BF16),
        scratch_shapes=[pltpu.VMEM((hp, nb, HEAD_DIM), F32),
                        pltpu.VMEM((hp, nb // group, HEAD_DIM, group * MOBA_BLOCK), BF16),
                        pltpu.VMEM((hp, nb + group, MOBA_BLOCK), F32),
                        pltpu.VMEM((hp, group * MOBA_BLOCK, MOBA_BLOCK), F32),
                        pltpu.VMEM((hp, group * MOBA_BLOCK, MOBA_BLOCK), F32)],
        compiler_params=_params("parallel", "arbitrary"),
    )(qkv, qkv, qkv)


def _mix_up_kernel(p_ref, a_ref, wp_ref, wa_ref, gp_ref, ga_ref, o_ref):
    y_pool = _dot(p_ref[...], wp_ref[...])
    y_attn = _dot(a_ref[...], wa_ref[...])
    o_ref[...] = (gp_ref[...] * y_pool + ga_ref[...] * y_attn).astype(o_ref.dtype)


def _mix_up(p, o, w_up_pool_b, w_up_attn_b, gates):
    S, PW = p.shape
    AW = o.shape[1]
    D = w_up_pool_b.shape[1]
    tm = _tile(S, 1024, 8)
    tn = _tile(D, 1024)
    nt = D // tn
    return pl.pallas_call(
        _mix_up_kernel, name="mix_up",
        grid=(S // tm, nt),
        in_specs=[pl.BlockSpec((tm, PW), lambda i, n: (i, 0)),
                  pl.BlockSpec((tm, AW), lambda i, n: (i, 0)),
                  pl.BlockSpec((PW, tn), lambda i, n: (0, n)),
                  pl.BlockSpec((AW, tn), lambda i, n: (0, n)),
                  pl.BlockSpec((tm, tn), lambda i, n: (i, n)),
                  pl.BlockSpec((tm, tn), lambda i, n: (i, nt + n))],
        out_specs=pl.BlockSpec((tm, tn), lambda i, n: (i, n)),
        out_shape=jax.ShapeDtypeStruct((S, D), BF16),
        compiler_params=_params("parallel", "arbitrary"),
    )(p, o, w_up_pool_b, w_up_attn_b, gates, gates)


def _mix_out_kernel(m_ref, w_ref, x_ref, g_ref, b_ref, xf_ref, xb_ref, *, alpha):
    mix = _dot(m_ref[...], w_ref[...])
    y = _layer_norm(alpha * x_ref[...] + mix, g_ref[...], b_ref[...])
    xf_ref[...] = y
    xb_ref[...] = y.astype(BF16)


def _mix_out(m, w_o_b, x, g, b, alpha):
    S, D = x.shape
    tm = _tile(S, 512, 8)
    row = lambda dt: pl.BlockSpec((tm, D), lambda i: (i, 0))
    vec = pl.BlockSpec((1, D), lambda i: (0, 0))
    return pl.pallas_call(
        functools.partial(_mix_out_kernel, alpha=alpha), name="mix_out",
        grid=(S // tm,),
        in_specs=[row(BF16), pl.BlockSpec((D, D), lambda i: (0, 0)), row(F32), vec, vec],
        out_specs=[row(F32), row(BF16)],
        out_shape=[jax.ShapeDtypeStruct((S, D), F32), jax.ShapeDtypeStruct((S, D), BF16)],
        compiler_params=_params("parallel"),
    )(m, w_o_b, x, g.reshape(1, D), b.reshape(1, D))


def _swiglu_kernel(nvalid_ref, eid_ref, x_ref, wg_ref, wu_ref, wd_ref, o_ref, *, sub, tail):
    c = pl.program_id(0)
    f = pl.program_id(1)
    nv = nvalid_ref[c]
    rows = x_ref.shape[0]

    @pl.when(f == 0)
    def _():
        o_ref[...] = jnp.zeros(o_ref.shape, F32)

    def run(n_rows):
        wg = wg_ref[0].astype(BF16)
        wu = wu_ref[0].astype(BF16)
        wd = wd_ref[0].astype(BF16)
        for s0 in range(0, n_rows, sub):
            rs = slice(s0, min(s0 + sub, n_rows))
            xs = x_ref[rs, :]
            gate = _dot(xs, wg)
            up = _dot(xs, wu)
            h = (gate * _sigmoid(gate) * up).astype(BF16)
            o_ref[rs, :] += _dot(h, wd)

    n_var = rows // tail
    for k in range(1, n_var + 1):
        lo = (k - 1) * tail
        cond = (nv > lo) if k == n_var else ((nv > lo) & (nv <= lo + tail))
        pl.when(cond)(functools.partial(run, k * tail))


def _swiglu(xrows, w_gate, w_up, w_down, nvalid, eid, w_base, chunk, tf_pref):
    R, D = xrows.shape
    F = w_gate.shape[-1]
    tf = _tile(F, tf_pref)
    sub = min(EXPERT_SUB, chunk)
    tail = min(EXPERT_TAIL, sub)
    assert R % chunk == 0 and chunk % sub == 0 and sub % tail == 0 and tail % GATHER_ROWS == 0

    def w_in_map(c, f, nv, e):
        return (w_base + e[c], 0, jnp.where(nv[c] > 0, f, 0))

    def w_out_map(c, f, nv, e):
        return (w_base + e[c], jnp.where(nv[c] > 0, f, 0), 0)

    grid_spec = pltpu.PrefetchScalarGridSpec(
        num_scalar_prefetch=2,
        grid=(R // chunk, F // tf),
        in_specs=[pl.BlockSpec((chunk, D), lambda c, f, nv, e: (c, 0)),
                  pl.BlockSpec((1, D, tf), w_in_map),
                  pl.BlockSpec((1, D, tf), w_in_map),
                  pl.BlockSpec((1, tf, D), w_out_map)],
        out_specs=pl.BlockSpec((chunk, D), lambda c, f, nv, e: (c, 0)),
    )
    return pl.pallas_call(
        functools.partial(_swiglu_kernel, sub=sub, tail=tail), name="swiglu",
        grid_spec=grid_spec,
        out_shape=jax.ShapeDtypeStruct((R, D), F32),
        compiler_params=_params("parallel", "arbitrary"),
    )(nvalid, eid, xrows, w_gate, w_up, w_down)


def _router_kernel(x_ref, rh_ref, rl_ref, o_ref):
    x = x_ref[...]
    xh = x.astype(BF16)
    xl = (x - xh.astype(F32)).astype(BF16)
    o_ref[...] = _dot(xh, rh_ref[...]) + _dot(xl, rh_ref[...]) + _dot(xh, rl_ref[...])


def _router(x, router_w):
    S, D = x.shape
    E = router_w.shape[1]
    wpad = jnp.pad(router_w, ((0, 0), (0, LANES - E)))
    rh = wpad.astype(BF16)
    rl = (wpad - rh.astype(F32)).astype(BF16)
    tm = _tile(S, 512, 8)
    wspec = pl.BlockSpec((D, LANES), lambda i: (0, 0))
    logits = pl.pallas_call(
        _router_kernel, name="router",
        grid=(S // tm,),
        in_specs=[pl.BlockSpec((tm, D), lambda i: (i, 0)), wspec, wspec],
        out_specs=pl.BlockSpec((tm, LANES), lambda i: (i, 0)),
        out_shape=jax.ShapeDtypeStruct((S, LANES), F32),
        compiler_params=_params("parallel"),
    )(x, rh, rl)
    return logits[:, :E]


def _gather_kernel(tok_ref, valid_ref, x_hbm, o_ref, buf_ref, sem, *, rows, n_blocks):
    b = pl.program_id(0)

    def start(blk, slot):
        base = blk * rows

        def issue(r8, carry):
            for j in range(DMA_ISSUE_UNROLL):
                r = r8 * DMA_ISSUE_UNROLL + j
                tok = tok_ref[base + r]
                pltpu.make_async_copy(x_hbm.at[pl.ds(tok, 1), :], buf_ref.at[slot, pl.ds(r, 1), :],
                                      sem.at[slot]).start(priority=j % 2)
            return carry

        lax.fori_loop(0, rows // DMA_ISSUE_UNROLL, issue, 0)

    @pl.when((b == 0) & (valid_ref[0] > 0))
    def _():
        start(0, 0)

    nxt = jnp.minimum(b + 1, n_blocks - 1)

    @pl.when((b + 1 < n_blocks) & (valid_ref[nxt] > 0))
    def _():
        start(nxt, nxt % 2)

    @pl.when(valid_ref[b] > 0)
    def _():
        slot = b % 2
        pltpu.make_async_copy(x_hbm.at[pl.ds(0, rows), :], buf_ref.at[slot], sem.at[slot]).wait()
        o_ref[...] = buf_ref[slot].astype(o_ref.dtype)

    @pl.when(valid_ref[b] == 0)
    def _():
        o_ref[...] = jnp.zeros(o_ref.shape, o_ref.dtype)


def _gather_rows(x, row_tok, blk_valid, n_rows):
    S, D = x.shape
    rows = GATHER_ROWS
    grid_spec = pltpu.PrefetchScalarGridSpec(
        num_scalar_prefetch=2,
        grid=(n_rows // rows,),
        in_specs=[pl.BlockSpec(memory_space=pl.ANY)],
        out_specs=pl.BlockSpec((rows, D), lambda b, t, v: (b, 0)),
        scratch_shapes=[pltpu.VMEM((2, rows, D), F32), pltpu.SemaphoreType.DMA((2,))],
    )
    return pl.pallas_call(
        functools.partial(_gather_kernel, rows=rows, n_blocks=n_rows // rows), name="gather_rows",
        grid_spec=grid_spec,
        out_shape=jax.ShapeDtypeStruct((n_rows, D), BF16),
        compiler_params=_params("arbitrary"),
    )(row_tok, blk_valid, x)


def _combine_kernel(pos_ref, y_hbm, w_ref, x_ref, g_ref, b_ref, xf_ref, xb_ref, buf_ref, sem,
                    *, rows, n_blocks, alpha):
    b = pl.program_id(0)

    def start(blk, slot):
        base = blk * rows * MOE_TOPK

        def issue(r8, carry):
            for j in range(DMA_ISSUE_UNROLL):
                r = r8 * DMA_ISSUE_UNROLL + j
                for k in range(MOE_TOPK):
                    src = pos_ref[base + r * MOE_TOPK + k]
                    pltpu.make_async_copy(y_hbm.at[pl.ds(src, 1), :], buf_ref.at[slot, k, pl.ds(r, 1), :],
                                          sem.at[slot]).start(priority=k % 2)
            return carry

        lax.fori_loop(0, rows // DMA_ISSUE_UNROLL, issue, 0)

    @pl.when(b == 0)
    def _():
        start(0, 0)

    @pl.when(b + 1 < n_blocks)
    def _():
        start(b + 1, (b + 1) % 2)

    slot = b % 2
    for k in range(MOE_TOPK):
        pltpu.make_async_copy(y_hbm.at[pl.ds(0, rows), :], buf_ref.at[slot, k], sem.at[slot]).wait()
    w = w_ref[...]
    f = w[:, 0:1] * buf_ref[slot, 0] + w[:, 1:2] * buf_ref[slot, 1]
    y = _layer_norm(alpha * x_ref[...] + f, g_ref[...], b_ref[...])
    xf_ref[...] = y
    xb_ref[...] = y.astype(BF16)


def _combine_ln(y, pos, top_w, x, g, b, alpha):
    S, D = x.shape
    rows = _tile(S, COMBINE_ROWS, 8)
    row = pl.BlockSpec((rows, D), lambda i, p: (i, 0))
    vec = pl.BlockSpec((1, D), lambda i, p: (0, 0))
    grid_spec = pltpu.PrefetchScalarGridSpec(
        num_scalar_prefetch=1,
        grid=(S // rows,),
        in_specs=[pl.BlockSpec(memory_space=pl.ANY),
                  pl.BlockSpec((rows, MOE_TOPK), lambda i, p: (i, 0)),
                  row, vec, vec],
        out_specs=[row, row],
        scratch_shapes=[pltpu.VMEM((2, MOE_TOPK, rows, D), F32), pltpu.SemaphoreType.DMA((2,))],
    )
    return pl.pallas_call(
        functools.partial(_combine_kernel, rows=rows, n_blocks=S // rows, alpha=alpha), name="combine_ln",
        grid_spec=grid_spec,
        out_shape=[jax.ShapeDtypeStruct((S, D), F32), jax.ShapeDtypeStruct((S, D), BF16)],
        compiler_params=_params("arbitrary"),
    )(pos, y, top_w, x, g.reshape(1, D), b.reshape(1, D))


def _route(logits, chunk):
    S, E = logits.shape
    top_logit, top_e = lax.top_k(logits, MOE_TOPK)
    top_w = jax.nn.softmax(top_logit, axis=-1)
    n_assign = S * MOE_TOPK
    flat_e = top_e.reshape(-1).astype(jnp.int32)
    onehot = (flat_e[:, None] == jnp.arange(E, dtype=jnp.int32)[None, :]).astype(jnp.int32)
    running = jnp.cumsum(onehot, axis=0)
    rank = jnp.sum(running * onehot, axis=1) - 1
    counts = running[-1]
    padded = (counts + chunk - 1) // chunk * chunk
    ends = jnp.cumsum(padded)
    pstart = ends - padded
    dest = pstart[flat_e] + rank
    n_rows = (-(-n_assign // chunk) + E) * chunk
    n_chunks = n_rows // chunk
    flat_tok = jnp.arange(n_assign, dtype=jnp.int32) // MOE_TOPK
    row_tok = jnp.zeros((n_rows,), jnp.int32).at[dest].set(
        flat_tok, unique_indices=True, mode="promise_in_bounds")
    chunk_start = jnp.arange(n_chunks, dtype=jnp.int32) * chunk
    eid = jnp.minimum(jnp.searchsorted(ends, chunk_start, side="right"), E - 1).astype(jnp.int32)
    nvalid = jnp.clip(counts[eid] - (chunk_start - pstart[eid]), 0, chunk).astype(jnp.int32)
    per = chunk // GATHER_ROWS
    blk = jnp.arange(n_rows // GATHER_ROWS, dtype=jnp.int32)
    blk_valid = (nvalid[blk // per] > (blk % per) * GATHER_ROWS).astype(jnp.int32)
    return top_w, dest.astype(jnp.int32), row_tok, eid, nvalid, blk_valid, n_rows


def kernel(x, ln_in_g, ln_in_b, w_in, pool_w, pool_scale, w_up_pool, w_up_attn, w_o, ln_mix_g, ln_mix_b, ffn_w_gate, ffn_w_up, ffn_w_down, moe_router, moe_w_gate, moe_w_up, moe_w_down, ln_ffn_g, ln_ffn_b):
    B, S, D = x.shape
    assert B == 1
    depth = w_in.shape[0]
    PW = w_up_pool.shape[1]
    AW = w_up_attn.shape[1]
    n_heads = AW // HEAD_DIM
    n_exp = moe_router.shape[-1]
    alpha = float((2 * depth) ** 0.25)

    pool_w_b = pool_w.astype(BF16)
    w_up_pool_b = w_up_pool.astype(BF16)
    w_up_attn_b = w_up_attn.astype(BF16)
    w_o_b = w_o.astype(BF16)
    moe_gate = moe_w_gate.reshape((-1,) + moe_w_gate.shape[2:])
    moe_up = moe_w_up.reshape((-1,) + moe_w_up.shape[2:])
    moe_down = moe_w_down.reshape((-1,) + moe_w_down.shape[2:])

    chunk = min(EXPERT_CHUNK, S)
    dense_nvalid = jnp.full((S // chunk,), chunk, jnp.int32)
    dense_eid = jnp.zeros((S // chunk,), jnp.int32)

    xf, xb = _ln_in(x.reshape(S, D), ln_in_g, ln_in_b)
    for l in range(depth):
        u = _proj(xb, w_in, l, 0, PW, F32, "plain")
        qkv = _proj(xb, w_in, l, PW, 3 * AW, BF16, "qkv", q_width=AW)
        gates = _proj(xb, w_in, l, PW + 3 * AW, 2 * D, BF16, "sigmoid")
        p = _pool(u, pool_w_b[l], pool_scale[l])
        o = _moba(qkv, n_heads)
        m = _mix_up(p, o, w_up_pool_b[l], w_up_attn_b[l], gates)
        xf, xb = _mix_out(m, w_o_b[l], xf, ln_mix_g[l], ln_mix_b[l], alpha)
        i = l // 2
        if l % 2 == 0:
            f = _swiglu(xb, ffn_w_gate, ffn_w_up, ffn_w_down, dense_nvalid, dense_eid, i, chunk, 512)
            xf, xb = _add_ln(xf, f, ln_ffn_g[l], ln_ffn_b[l], alpha)
        else:
            logits = _router(xf, moe_router[i])
            top_w, dest, row_tok, eid, nvalid, blk_valid, n_rows = _route(logits, chunk)
            xg = _gather_rows(xf, row_tok, blk_valid, n_rows)
            y = _swiglu(xg, moe_gate, moe_up, moe_down, nvalid, eid, i * n_exp, chunk, 512)
            xf, xb = _combine_ln(y, dest, top_w, xf, ln_ffn_g[l], ln_ffn_b[l], alpha)
    return xf.reshape(B, S, D)
```

```python
import functools

import jax
import jax.numpy as jnp
import numpy as np
from jax import lax
from jax.experimental import pallas as pl
from jax.experimental.pallas import tpu as pltpu

F32 = jnp.float32
BF16 = jnp.bfloat16

HEAD_DIM = 128
MOBA_BLOCK = 256
MOBA_TOPK = 3
MOBA_GROUP = 2
MOBA_QBLOCKS_PER_STEP = 2
MOBA_HEADS_PER_STEP = 4
POOL_WINDOWS = (2, 4, 8, 16)
POOL_HALO = 16
MOE_TOPK = 2
LN_EPS = 1e-5
LOG2_E = float(np.log2(np.e))
LANES = 128
VMEM_LIMIT = 56 * 1024 * 1024

EXPERT_CHUNK = 1024
EXPERT_SUB = 512
EXPERT_TAIL = 256
GATHER_ROWS = 256
COMBINE_ROWS = 512
DMA_ISSUE_UNROLL = 8


def _tile(n, pref, mult=LANES):
    if n <= pref:
        return n
    t = (pref // mult) * mult
    while t > mult and n % t:
        t -= mult
    assert n % t == 0, (n, pref)
    return t


def _params(*sem):
    return pltpu.CompilerParams(dimension_semantics=sem, vmem_limit_bytes=VMEM_LIMIT)


def _layer_norm(v, g, b):
    mu = jnp.mean(v, axis=-1, keepdims=True)
    d = v - mu
    var = jnp.mean(d * d, axis=-1, keepdims=True)
    return d * lax.rsqrt(var + LN_EPS) * g + b


def _sigmoid(v):
    return 1.0 / (1.0 + jnp.exp(-v))


def _dot(a, b):
    return jnp.dot(a, b, preferred_element_type=F32)


def _dot_nt(a, b):
    return lax.dot_general(a, b, (((1,), (1,)), ((), ())), preferred_element_type=F32)


def _ln_kernel(x_ref, g_ref, b_ref, xf_ref, xb_ref):
    y = _layer_norm(x_ref[...], g_ref[...], b_ref[...])
    xf_ref[...] = y
    xb_ref[...] = y.astype(BF16)


def _ln_in(x, g, b):
    S, D = x.shape
    tm = _tile(S, 512, 8)
    row = pl.BlockSpec((tm, D), lambda i: (i, 0))
    vec = pl.BlockSpec((1, D), lambda i: (0, 0))
    return pl.pallas_call(
        _ln_kernel, name="ln_in",
        grid=(S // tm,),
        in_specs=[row, vec, vec],
        out_specs=[row, row],
        out_shape=[jax.ShapeDtypeStruct((S, D), F32), jax.ShapeDtypeStruct((S, D), BF16)],
        compiler_params=_params("parallel"),
    )(x, g.reshape(1, D), b.reshape(1, D))


def _add_ln_kernel(x_ref, f_ref, g_ref, b_ref, xf_ref, xb_ref, *, alpha):
    y = _layer_norm(alpha * x_ref[...] + f_ref[...], g_ref[...], b_ref[...])
    xf_ref[...] = y
    xb_ref[...] = y.astype(BF16)


def _add_ln(x, f, g, b, alpha):
    S, D = x.shape
    tm = _tile(S, 512, 8)
    row = pl.BlockSpec((tm, D), lambda i: (i, 0))
    vec = pl.BlockSpec((1, D), lambda i: (0, 0))
    return pl.pallas_call(
        functools.partial(_add_ln_kernel, alpha=alpha), name="add_ln",
        grid=(S // tm,),
        in_specs=[row, row, vec, vec],
        out_specs=[row, row],
        out_shape=[jax.ShapeDtypeStruct((S, D), F32), jax.ShapeDtypeStruct((S, D), BF16)],
        compiler_params=_params("parallel"),
    )(x, f, g.reshape(1, D), b.reshape(1, D))


def _proj_kernel(a_ref, b_ref, o_ref, bb_ref, *, mode, q_tiles, q_scale):
    @pl.when(pl.program_id(1) == 0)
    def _():
        bb_ref[...] = b_ref[0].astype(BF16)

    acc = _dot(a_ref[...], bb_ref[...])
    if mode == "qkv":
        acc = acc * jnp.where(pl.program_id(0) < q_tiles, q_scale, 1.0).astype(F32)
    elif mode == "sigmoid":
        acc = _sigmoid(acc)
    o_ref[...] = acc.astype(o_ref.dtype)


def _proj(xb, w_in, layer, col0, width, out_dtype, mode, q_width=0):
    S, D = xb.shape
    tm = _tile(S, 1024, 8)
    tn = _tile(int(np.gcd.reduce([width, col0 or width, q_width or width])), 1024)
    assert col0 % tn == 0 and width % tn == 0 and q_width % tn == 0
    off = col0 // tn
    kern = functools.partial(_proj_kernel, mode=mode, q_tiles=q_width // tn,
                             q_scale=HEAD_DIM ** -0.5 * LOG2_E)
    return pl.pallas_call(
        kern, name="proj_" + mode,
        grid=(width // tn, S // tm),
        in_specs=[pl.BlockSpec((tm, D), lambda n, i: (i, 0)),
                  pl.BlockSpec((1, D, tn), lambda n, i: (layer, 0, off + n))],
        out_specs=pl.BlockSpec((tm, tn), lambda n, i: (i, n)),
        out_shape=jax.ShapeDtypeStruct((S, width), out_dtype),
        scratch_shapes=[pltpu.VMEM((D, tn), BF16)],
        compiler_params=_params("parallel", "arbitrary"),
    )(xb, w_in)


def _pool_kernel(u_ref, uprev_ref, pw_ref, sc_ref, o_ref, ext_ref, *, tp, group):
    i = pl.program_id(0)
    halo = uprev_ref[...]
    ext_ref[0:POOL_HALO, :] = jnp.where(i > 0, halo, jnp.zeros_like(halo))
    ext_ref[POOL_HALO:, :] = u_ref[...]
    t = i * tp + lax.broadcasted_iota(jnp.int32, (tp, 1), 0)
    for g, w in enumerate(POOL_WINDOWS):
        cols = slice(g * group, (g + 1) * group)
        s = ext_ref[:, cols]
        k = 1
        while k < w:
            s = s + pltpu.roll(s, k, 0)
            k *= 2
        cnt = jnp.minimum(t + 1, w).astype(F32)
        cur = u_ref[:, cols]
        pooled = s[POOL_HALO:, :] / cnt - cur
        y = _dot(pooled.astype(BF16), pw_ref[g]) * sc_ref[:, cols]
        o_ref[:, cols] = y.astype(o_ref.dtype)


def _pool(u, pool_w_b, pool_scale):
    S, PW = u.shape
    G, C, _ = pool_w_b.shape
    assert G == len(POOL_WINDOWS) and G * C == PW
    tp = _tile(S, 512, POOL_HALO)
    per = tp // POOL_HALO
    return pl.pallas_call(
        functools.partial(_pool_kernel, tp=tp, group=C), name="pool",
        grid=(S // tp,),
        in_specs=[pl.BlockSpec((tp, PW), lambda i: (i, 0)),
                  pl.BlockSpec((POOL_HALO, PW), lambda i: (jnp.maximum(i * per - 1, 0), 0)),
                  pl.BlockSpec((G, C, C), lambda i: (0, 0, 0)),
                  pl.BlockSpec((1, PW), lambda i: (0, 0))],
        out_specs=pl.BlockSpec((tp, PW), lambda i: (i, 0)),
        out_shape=jax.ShapeDtypeStruct((S, PW), BF16),
        scratch_shapes=[pltpu.VMEM((tp + POOL_HALO, PW), F32)],
        compiler_params=_params("parallel"),
    )(u, u, pool_w_b, pool_scale.reshape(1, PW))


def _moba_kernel(q_ref, k_ref, v_ref, o_ref, *scratch, nb, group, heads, qblocks):
    step = pl.program_id(1)
    for qb in range(qblocks):
        rows = pl.ds(qb * MOBA_BLOCK, MOBA_BLOCK)
        _moba_block(step * qblocks + qb, q_ref.at[rows, :], k_ref, v_ref, o_ref.at[rows, :], *scratch,
                    nb=nb, group=group, heads=heads)


def _moba_block(i, q_ref, k_ref, v_ref, o_ref, kmean_ref, vt_ref, bias_ref, sa_ref, sb_ref,
                *, nb, group, heads):
    B = MOBA_BLOCK
    neg = jnp.float32(-jnp.inf)
    U = group
    last = nb // U - 1
    lanes = [slice(h * HEAD_DIM, (h + 1) * HEAD_DIM) for h in range(heads)]

    @pl.when(i == 0)
    def _():
        def prep(g, carry):
            for h in range(heads):
                for u in range(U):
                    rj = pl.multiple_of((g * U + u) * B, B)
                    vt_ref[h, g, :, u * B:(u + 1) * B] = (
                        v_ref[pl.ds(rj, B), lanes[h]].astype(F32).T.astype(BF16))
                    kj = k_ref[pl.ds(rj, B), lanes[h]].astype(F32)
                    kmean_ref[h, pl.ds(g * U + u, 1), :] = jnp.sum(kj, axis=0, keepdims=True) / B
            return carry

        lax.fori_loop(0, nb // U, prep, 0)

    row0 = pl.multiple_of(i * B, B)
    blk = lax.broadcasted_iota(jnp.int32, (nb, B), 0)
    key_id = lax.broadcasted_iota(jnp.int32, (B, B), 0)
    qry_id = lax.broadcasted_iota(jnp.int32, (B, B), 1)
    qts = [q_ref[:, lanes[h]].astype(F32).T.astype(BF16) for h in range(heads)]

    def scores(h, g):
        r0 = pl.multiple_of(jnp.minimum(g, last) * (U * B), U * B)
        return _dot(k_ref[pl.ds(r0, U * B), lanes[h]], qts[h])

    gates, s_own, vt_own = [], [], []
    for h in range(heads):
        km = kmean_ref[h]
        km_hi = km.astype(BF16)
        km_lo = (km - km_hi.astype(F32)).astype(BF16)
        gates.append(_dot(km_hi, qts[h]) + _dot(km_lo, qts[h]))
        s_own.append(_dot(k_ref[pl.ds(row0, B), lanes[h]], qts[h]))
        vt_own.append(v_ref[pl.ds(row0, B), lanes[h]].astype(F32).T.astype(BF16))
    for h in range(heads):
        sa_ref[h] = scores(h, 0)

    init = []
    for h in range(heads):
        g = jnp.where(blk < i, gates[h], neg)
        bias = jnp.full((nb, B), neg, F32)
        for _ in range(MOBA_TOPK):
            m = jnp.max(g, axis=0, keepdims=True)
            idx = jnp.min(jnp.where(g == m, blk, nb), axis=0, keepdims=True)
            hit = blk == idx
            bias = jnp.where(hit & (m > neg), 0.0, bias)
            g = jnp.where(hit, neg, g)
        bias_ref[h, 0:nb, :] = bias
        bias_ref[h, nb:, :] = jnp.full((U, B), neg, F32)

        s = jnp.where(key_id <= qry_id, s_own[h], neg)
        m0 = jnp.max(s, axis=0, keepdims=True)
        p = jnp.exp2(s - m0)
        l0 = jnp.sum(p, axis=0, keepdims=True)
        init.append((m0, l0, p.astype(BF16)))
    init = [(m0, l0, _dot(vt_own[h], p)) for h, (m0, l0, p) in enumerate(init)]

    def process(h, s_ref, g, state):
        m_run, l_run, acc = state
        sel_u = [bias_ref[h, pl.ds(g * U + u, 1), :] for u in range(U)]
        s_u = [s_ref[h, u * B:(u + 1) * B, :] for u in range(U)]
        m_new = m_run
        for u in range(U):
            m_new = jnp.maximum(m_new, jnp.max(s_u[u], axis=0, keepdims=True) + sel_u[u])
        a = jnp.exp2(m_run - m_new)
        l_new = a * l_run
        acc_new = a * acc
        gv = jnp.minimum(g, last)
        for u in range(U):
            p = jnp.exp2(s_u[u] - (m_new - sel_u[u]))
            l_new = l_new + jnp.sum(p, axis=0, keepdims=True)
            acc_new = acc_new + _dot(vt_ref[h, gv, :, u * B:(u + 1) * B], p.astype(BF16))
        return m_new, l_new, acc_new

    def pair(t, states):
        for h in range(heads):
            sb_ref[h] = scores(h, 2 * t + 1)
        states = tuple(process(h, sa_ref, 2 * t, states[h]) for h in range(heads))
        for h in range(heads):
            sa_ref[h] = scores(h, 2 * t + 2)
        return tuple(process(h, sb_ref, 2 * t + 1, states[h]) for h in range(heads))

    n_groups = (i + U - 1) // U
    final = lax.fori_loop(0, n_groups // 2, pair, tuple(init))
    final = lax.cond(
        n_groups % 2 == 1,
        lambda st: tuple(process(h, sa_ref, n_groups - 1, st[h]) for h in range(heads)),
        lambda st: st,
        final)
    for h in range(heads):
        _, l_fin, acc_fin = final[h]
        o_ref[:, lanes[h]] = (acc_fin / l_fin).T.astype(o_ref.dtype)


def _moba(qkv, n_heads):
    S = qkv.shape[0]
    assert S % MOBA_BLOCK == 0
    nb = S // MOBA_BLOCK
    H = n_heads
    group = MOBA_GROUP
    hp = MOBA_HEADS_PER_STEP
    qbs = MOBA_QBLOCKS_PER_STEP
    assert nb % group == 0 and H % hp == 0 and nb % qbs == 0
    G = H // hp
    W = hp * HEAD_DIM
    return pl.pallas_call(
        functools.partial(_moba_kernel, nb=nb, group=group, heads=hp, qblocks=qbs), name="moba",
        grid=(G, nb // qbs),
        in_specs=[pl.BlockSpec((qbs * MOBA_BLOCK, W), lambda h, i: (i, h)),
                  pl.BlockSpec((S, W), lambda h, i: (0, G + h)),
                  pl.BlockSpec((S, W), lambda h, i: (0, 2 * G + h))],
        out_specs=pl.BlockSpec((qbs * MOBA_BLOCK, W), lambda h, i: (i, h)),
        out_shape=jax.ShapeDtypeStruct((S, H * HEAD_DIM), BF16),
        scratch_shapes=[pltpu.VMEM((hp, nb, HEAD_DIM), F32),
                        pltpu.VMEM((hp, nb // group, HEAD_DIM, group * MOBA_BLOCK), BF16),
                        pltpu.VMEM((hp, nb + group, MOBA_BLOCK), F32),
                        pltpu.VMEM((hp, group * MOBA_BLOCK, MOBA_BLOCK), F32),
                        pltpu.VMEM((hp, group * MOBA_BLOCK, MOBA_BLOCK), F32)],
        compiler_params=_params("parallel", "arbitrary"),
    )(qkv, qkv, qkv)


def _mix_up_kernel(p_ref, a_ref, wp_ref, wa_ref, gp_ref, ga_ref, o_ref):
    y_pool = _dot(p_ref[...], wp_ref[...])
    y_attn = _dot(a_ref[...], wa_ref[...])
    o_ref[...] = (gp_ref[...] * y_pool + ga_ref[...] * y_attn).astype(o_ref.dtype)


def _mix_up(p, o, w_up_pool_b, w_up_attn_b, gates):
    S, PW = p.shape
    AW = o.shape[1]
    D = w_up_pool_b.shape[1]
    tm = _tile(S, 1024, 8)
    tn = _tile(D, 1024)
    nt = D // tn
    return pl.pallas_call(
        _mix_up_kernel, name="mix_up",
        grid=(S // tm, nt),
        in_specs=[pl.BlockSpec((tm, PW), lambda i, n: (i, 0)),
                  pl.BlockSpec((tm, AW), lambda i, n: (i, 0)),
                  pl.BlockSpec((PW, tn), lambda i, n: (0, n)),
                  pl.BlockSpec((AW, tn), lambda i, n: (0, n)),
                  pl.BlockSpec((tm, tn), lambda i, n: (i, n)),
                  pl.BlockSpec((tm, tn), lambda i, n: (i, nt + n))],
        out_specs=pl.BlockSpec((tm, tn), lambda i, n: (i, n)),
        out_shape=jax.ShapeDtypeStruct((S, D), BF16),
        compiler_params=_params("parallel", "arbitrary"),
    )(p, o, w_up_pool_b, w_up_attn_b, gates, gates)


def _mix_out_kernel(m_ref, w_ref, x_ref, g_ref, b_ref, xf_ref, xb_ref, *, alpha):
    mix = _dot(m_ref[...], w_ref[...])
    y = _layer_norm(alpha * x_ref[...] + mix, g_ref[...], b_ref[...])
    xf_ref[...] = y
    xb_ref[...] = y.astype(BF16)


def _mix_out(m, w_o_b, x, g, b, alpha):
    S, D = x.shape
    tm = _tile(S, 512, 8)
    row = lambda dt: pl.BlockSpec((tm, D), lambda i: (i, 0))
    vec = pl.BlockSpec((1, D), lambda i: (0, 0))
    return pl.pallas_call(
        functools.partial(_mix_out_kernel, alpha=alpha), name="mix_out",
        grid=(S // tm,),
        in_specs=[row(BF16), pl.BlockSpec((D, D), lambda i: (0, 0)), row(F32), vec, vec],
        out_specs=[row(F32), row(BF16)],
        out_shape=[jax.ShapeDtypeStruct((S, D), F32), jax.ShapeDtypeStruct((S, D), BF16)],
        compiler_params=_params("parallel"),
    )(m, w_o_b, x, g.reshape(1, D), b.reshape(1, D))


def _swiglu_kernel(nvalid_ref, eid_ref, x_ref, wg_ref, wu_ref, wd_ref, o_ref, *, sub, tail):
    c = pl.program_id(0)
    f = pl.program_id(1)
    nv = nvalid_ref[c]
    rows = x_ref.shape[0]

    @pl.when(f == 0)
    def _():
        o_ref[...] = jnp.zeros(o_ref.shape, F32)

    def run(n_rows):
        wg = wg_ref[0].astype(BF16)
        wu = wu_ref[0].astype(BF16)
        wd = wd_ref[0].astype(BF16)
        for s0 in range(0, n_rows, sub):
            rs = slice(s0, min(s0 + sub, n_rows))
            xs = x_ref[rs, :]
            gate = _dot(xs, wg)
            up = _dot(xs, wu)
            h = (gate * _sigmoid(gate) * up).astype(BF16)
            o_ref[rs, :] += _dot(h, wd)

    n_var = rows // tail
    for k in range(1, n_var + 1):
        lo = (k - 1) * tail
        cond = (nv > lo) if k == n_var else ((nv > lo) & (nv <= lo + tail))
        pl.when(cond)(functools.partial(run, k * tail))


def _swiglu(xrows, w_gate, w_up, w_down, nvalid, eid, w_base, chunk, tf_pref):
    R, D = xrows.shape
    F = w_gate.shape[-1]
    tf = _tile(F, tf_pref)
    sub = min(EXPERT_SUB, chunk)
    tail = min(EXPERT_TAIL, sub)
    assert R % chunk == 0 and chunk % sub == 0 and sub % tail == 0 and tail % GATHER_ROWS == 0

    def w_in_map(c, f, nv, e):
        return (w_base + e[c], 0, jnp.where(nv[c] > 0, f, 0))

    def w_out_map(c, f, nv, e):
        return (w_base + e[c], jnp.where(nv[c] > 0, f, 0), 0)

    grid_spec = pltpu.PrefetchScalarGridSpec(
        num_scalar_prefetch=2,
        grid=(R // chunk, F // tf),
        in_specs=[pl.BlockSpec((chunk, D), lambda c, f, nv, e: (c, 0)),
                  pl.BlockSpec((1, D, tf), w_in_map),
                  pl.BlockSpec((1, D, tf), w_in_map),
                  pl.BlockSpec((1, tf, D), w_out_map)],
        out_specs=pl.BlockSpec((chunk, D), lambda c, f, nv, e: (c, 0)),
    )
    return pl.pallas_call(
        functools.partial(_swiglu_kernel, sub=sub, tail=tail), name="swiglu",
        grid_spec=grid_spec,
        out_shape=jax.ShapeDtypeStruct((R, D), F32),
        compiler_params=_params("parallel", "arbitrary"),
    )(nvalid, eid, xrows, w_gate, w_up, w_down)


def _router_kernel(x_ref, rh_ref, rl_ref, o_ref):
    x = x_ref[...]
    xh = x.astype(BF16)
    xl = (x - xh.astype(F32)).astype(BF16)
    o_ref[...] = _dot(xh, rh_ref[...]) + _dot(xl, rh_ref[...]) + _dot(xh, rl_ref[...])


def _router(x, router_w):
    S, D = x.shape
    E = router_w.shape[1]
    wpad = jnp.pad(router_w, ((0, 0), (0, LANES - E)))
    rh = wpad.astype(BF16)
    rl = (wpad - rh.astype(F32)).astype(BF16)
    tm = _tile(S, 512, 8)
    wspec = pl.BlockSpec((D, LANES), lambda i: (0, 0))
    logits = pl.pallas_call(
        _router_kernel, name="router",
        grid=(S // tm,),
        in_specs=[pl.BlockSpec((tm, D), lambda i: (i, 0)), wspec, wspec],
        out_specs=pl.BlockSpec((tm, LANES), lambda i: (i, 0)),
        out_shape=jax.ShapeDtypeStruct((S, LANES), F32),
        compiler_params=_params("parallel"),
    )(x, rh, rl)
    return logits[:, :E]


def _gather_kernel(tok_ref, valid_ref, x_hbm, o_ref, buf_ref, sem, *, rows, n_blocks):
    b = pl.program_id(0)

    def start(blk, slot):
        base = blk * rows

        def issue(r, carry):
            tok = tok_ref[base + r]
            pltpu.make_async_copy(x_hbm.at[pl.ds(tok, 1), :], buf_ref.at[slot, pl.ds(r, 1), :],
                                  sem.at[slot]).start()
            return carry

        lax.fori_loop(0, rows, issue, 0, unroll=DMA_ISSUE_UNROLL)

    @pl.when((b == 0) & (valid_ref[0] > 0))
    def _():
        start(0, 0)

    nxt = jnp.minimum(b + 1, n_blocks - 1)

    @pl.when((b + 1 < n_blocks) & (valid_ref[nxt] > 0))
    def _():
        start(nxt, nxt % 2)

    @pl.when(valid_ref[b] > 0)
    def _():
        slot = b % 2
        pltpu.make_async_copy(x_hbm.at[pl.ds(0, rows), :], buf_ref.at[slot], sem.at[slot]).wait()
        o_ref[...] = buf_ref[slot].astype(o_ref.dtype)

    @pl.when(valid_ref[b] == 0)
    def _():
        o_ref[...] = jnp.zeros(o_ref.shape, o_ref.dtype)


def _gather_rows(x, row_tok, blk_valid, n_rows):
    S, D = x.shape
    rows = GATHER_ROWS
    grid_spec = pltpu.PrefetchScalarGridSpec(
        num_scalar_prefetch=2,
        grid=(n_rows // rows,),
        in_specs=[pl.BlockSpec(memory_space=pl.ANY)],
        out_specs=pl.BlockSpec((rows, D), lambda b, t, v: (b, 0)),
        scratch_shapes=[pltpu.VMEM((2, rows, D), F32), pltpu.SemaphoreType.DMA((2,))],
    )
    return pl.pallas_call(
        functools.partial(_gather_kernel, rows=rows, n_blocks=n_rows // rows), name="gather_rows",
        grid_spec=grid_spec,
        out_shape=jax.ShapeDtypeStruct((n_rows, D), BF16),
        compiler_params=_params("arbitrary"),
    )(row_tok, blk_valid, x)


def _combine_kernel(pos_ref, y_hbm, w_ref, x_ref, g_ref, b_ref, xf_ref, xb_ref, buf_ref, sem,
                    *, rows, n_blocks, alpha):
    b = pl.program_id(0)

    def start(blk, slot):
        base = blk * rows * MOE_TOPK

        def issue(r, carry):
            for k in range(MOE_TOPK):
                src = pos_ref[base + r * MOE_TOPK + k]
                pltpu.make_async_copy(y_hbm.at[pl.ds(src, 1), :], buf_ref.at[slot, k, pl.ds(r, 1), :],
                                      sem.at[slot]).start()
            return carry

        lax.fori_loop(0, rows, issue, 0, unroll=DMA_ISSUE_UNROLL)

    @pl.when(b == 0)
    def _():
        start(0, 0)

    @pl.when(b + 1 < n_blocks)
    def _():
        start(b + 1, (b + 1) % 2)

    slot = b % 2
    for k in range(MOE_TOPK):
        pltpu.make_async_copy(y_hbm.at[pl.ds(0, rows), :], buf_ref.at[slot, k], sem.at[slot]).wait()
    w = w_ref[...]
    f = w[:, 0:1] * buf_ref[slot, 0] + w[:, 1:2] * buf_ref[slot, 1]
    y = _layer_norm(alpha * x_ref[...] + f, g_ref[...], b_ref[...])
    xf_ref[...] = y
    xb_ref[...] = y.astype(BF16)


def _combine_ln(y, pos, top_w, x, g, b, alpha):
    S, D = x.shape
    rows = _tile(S, COMBINE_ROWS, 8)
    row = pl.BlockSpec((rows, D), lambda i, p: (i, 0))
    vec = pl.BlockSpec((1, D), lambda i, p: (0, 0))
    grid_spec = pltpu.PrefetchScalarGridSpec(
        num_scalar_prefetch=1,
        grid=(S // rows,),
        in_specs=[pl.BlockSpec(memory_space=pl.ANY),
                  pl.BlockSpec((rows, MOE_TOPK), lambda i, p: (i, 0)),
                  row, vec, vec],
        out_specs=[row, row],
        scratch_shapes=[pltpu.VMEM((2, MOE_TOPK, rows, D), F32), pltpu.SemaphoreType.DMA((2,))],
    )
    return pl.pallas_call(
        functools.partial(_combine_kernel, rows=rows, n_blocks=S // rows, alpha=alpha), name="combine_ln",
        grid_spec=grid_spec,
        out_shape=[jax.ShapeDtypeStruct((S, D), F32), jax.ShapeDtypeStruct((S, D), BF16)],
        compiler_params=_params("arbitrary"),
    )(pos, y, top_w, x, g.reshape(1, D), b.reshape(1, D))


def _route(logits, chunk):
    S, E = logits.shape
    top_logit, top_e = lax.top_k(logits, MOE_TOPK)
    top_w = jax.nn.softmax(top_logit, axis=-1)
    n_assign = S * MOE_TOPK
    flat_e = top_e.reshape(-1).astype(jnp.int32)
    onehot = (flat_e[:, None] == jnp.arange(E, dtype=jnp.int32)[None, :]).astype(jnp.int32)
    running = jnp.cumsum(onehot, axis=0)
    rank = jnp.sum(running * onehot, axis=1) - 1
    counts = running[-1]
    padded = (counts + chunk - 1) // chunk * chunk
    ends = jnp.cumsum(padded)
    pstart = ends - padded
    dest = pstart[flat_e] + rank
    n_rows = (-(-n_assign // chunk) + E) * chunk
    n_chunks = n_rows // chunk
    flat_tok = jnp.arange(n_assign, dtype=jnp.int32) // MOE_TOPK
    row_tok = jnp.zeros((n_rows,), jnp.int32).at[dest].set(flat_tok)
    chunk_start = jnp.arange(n_chunks, dtype=jnp.int32) * chunk
    eid = jnp.minimum(jnp.searchsorted(ends, chunk_start, side="right"), E - 1).astype(jnp.int32)
    nvalid = jnp.clip(counts[eid] - (chunk_start - pstart[eid]), 0, chunk).astype(jnp.int32)
    per = chunk // GATHER_ROWS
    blk = jnp.arange(n_rows // GATHER_ROWS, dtype=jnp.int32)
    blk_valid = (nvalid[blk // per] > (blk % per) * GATHER_ROWS).astype(jnp.int32)
    return top_w, dest.astype(jnp.int32), row_tok, eid, nvalid, blk_valid, n_rows


def kernel(x, ln_in_g, ln_in_b, w_in, pool_w, pool_scale, w_up_pool, w_up_attn, w_o, ln_mix_g, ln_mix_b, ffn_w_gate, ffn_w_up, ffn_w_down, moe_router, moe_w_gate, moe_w_up, moe_w_down, ln_ffn_g, ln_ffn_b):
    B, S, D = x.shape
    assert B == 1
    depth = w_in.shape[0]
    PW = w_up_pool.shape[1]
    AW = w_up_attn.shape[1]
    n_heads = AW // HEAD_DIM
    n_exp = moe_router.shape[-1]
    alpha = float((2 * depth) ** 0.25)

    pool_w_b = pool_w.astype(BF16)
    w_up_pool_b = w_up_pool.astype(BF16)
    w_up_attn_b = w_up_attn.astype(BF16)
    w_o_b = w_o.astype(BF16)
    moe_gate = moe_w_gate.reshape((-1,) + moe_w_gate.shape[2:])
    moe_up = moe_w_up.reshape((-1,) + moe_w_up.shape[2:])
    moe_down = moe_w_down.reshape((-1,) + moe_w_down.shape[2:])

    chunk = min(EXPERT_CHUNK, S)
    dense_nvalid = jnp.full((S // chunk,), chunk, jnp.int32)
    dense_eid = jnp.zeros((S // chunk,), jnp.int32)

    xf, xb = _ln_in(x.reshape(S, D), ln_in_g, ln_in_b)
    for l in range(depth):
        u = _proj(xb, w_in, l, 0, PW, F32, "plain")
        qkv = _proj(xb, w_in, l, PW, 3 * AW, BF16, "qkv", q_width=AW)
        gates = _proj(xb, w_in, l, PW + 3 * AW, 2 * D, BF16, "sigmoid")
        p = _pool(u, pool_w_b[l], pool_scale[l])
        o = _moba(qkv, n_heads)
        m = _mix_up(p, o, w_up_pool_b[l], w_up_attn_b[l], gates)
        xf, xb = _mix_out(m, w_o_b[l], xf, ln_mix_g[l], ln_mix_b[l], alpha)
        i = l // 2
        if l % 2 == 0:
            f = _swiglu(xb, ffn_w_gate, ffn_w_up, ffn_w_down, dense_nvalid, dense_eid, i, chunk, 512)
            xf, xb = _add_ln(xf, f, ln_ffn_g[l], ln_ffn_b[l], alpha)
        else:
            logits = _router(xf, moe_router[i])
            top_w, dest, row_tok, eid, nvalid, blk_valid, n_rows = _route(logits, chunk)
            xg = _gather_rows(xf, row_tok, blk_valid, n_rows)
            y = _swiglu(xg, moe_gate, moe_up, moe_down, nvalid, eid, i * n_exp, chunk, 512)
            xf, xb = _combine_ln(y, dest, top_w, xf, ln_ffn_g[l], ln_ffn_b[l], alpha)
    return xf.reshape(B, S, D)
```

```python
import functools

import jax
import jax.numpy as jnp
import numpy as np
from jax import lax
from jax.experimental import pallas as pl
from jax.experimental.pallas import tpu as pltpu

F32 = jnp.float32
BF16 = jnp.bfloat16

HEAD_DIM = 128
MOBA_BLOCK = 256
MOBA_TOPK = 3
MOBA_GROUP = 2
MOBA_QBLOCKS_PER_STEP = 2
MOBA_HEADS_PER_STEP = 4
POOL_WINDOWS = (2, 4, 8, 16)
POOL_HALO = 16
MOE_TOPK = 2
LN_EPS = 1e-5
LOG2_E = float(np.log2(np.e))
LANES = 128
VMEM_LIMIT = 56 * 1024 * 1024

EXPERT_CHUNK = 1024
EXPERT_SUB = 512
EXPERT_TAIL = 256
GATHER_ROWS = 256
COMBINE_ROWS = 512
DMA_ISSUE_UNROLL = 8


def _tile(n, pref, mult=LANES):
    if n <= pref:
        return n
    t = (pref // mult) * mult
    while t > mult and n % t:
        t -= mult
    assert n % t == 0, (n, pref)
    return t


def _params(*sem):
    return pltpu.CompilerParams(dimension_semantics=sem, vmem_limit_bytes=VMEM_LIMIT)


def _layer_norm(v, g, b):
    mu = jnp.mean(v, axis=-1, keepdims=True)
    d = v - mu
    var = jnp.mean(d * d, axis=-1, keepdims=True)
    return d * lax.rsqrt(var + LN_EPS) * g + b


def _sigmoid(v):
    return 1.0 / (1.0 + jnp.exp(-v))


def _dot(a, b):
    return jnp.dot(a, b, preferred_element_type=F32)


def _dot_nt(a, b):
    return lax.dot_general(a, b, (((1,), (1,)), ((), ())), preferred_element_type=F32)


def _ln_kernel(x_ref, g_ref, b_ref, xf_ref, xb_ref):
    y = _layer_norm(x_ref[...], g_ref[...], b_ref[...])
    xf_ref[...] = y
    xb_ref[...] = y.astype(BF16)


def _ln_in(x, g, b):
    S, D = x.shape
    tm = _tile(S, 512, 8)
    row = pl.BlockSpec((tm, D), lambda i: (i, 0))
    vec = pl.BlockSpec((1, D), lambda i: (0, 0))
    return pl.pallas_call(
        _ln_kernel, name="ln_in",
        grid=(S // tm,),
        in_specs=[row, vec, vec],
        out_specs=[row, row],
        out_shape=[jax.ShapeDtypeStruct((S, D), F32), jax.ShapeDtypeStruct((S, D), BF16)],
        compiler_params=_params("parallel"),
    )(x, g.reshape(1, D), b.reshape(1, D))


def _add_ln_kernel(x_ref, f_ref, g_ref, b_ref, xf_ref, xb_ref, *, alpha):
    y = _layer_norm(alpha * x_ref[...] + f_ref[...], g_ref[...], b_ref[...])
    xf_ref[...] = y
    xb_ref[...] = y.astype(BF16)


def _add_ln(x, f, g, b, alpha):
    S, D = x.shape
    tm = _tile(S, 512, 8)
    row = pl.BlockSpec((tm, D), lambda i: (i, 0))
    vec = pl.BlockSpec((1, D), lambda i: (0, 0))
    return pl.pallas_call(
        functools.partial(_add_ln_kernel, alpha=alpha), name="add_ln",
        grid=(S // tm,),
        in_specs=[row, row, vec, vec],
        out_specs=[row, row],
        out_shape=[jax.ShapeDtypeStruct((S, D), F32), jax.ShapeDtypeStruct((S, D), BF16)],
        compiler_params=_params("parallel"),
    )(x, f, g.reshape(1, D), b.reshape(1, D))


def _proj_kernel(a_ref, b_ref, o_ref, bb_ref, *, mode, q_tiles, q_scale):
    @pl.when(pl.program_id(1) == 0)
    def _():
        bb_ref[...] = b_ref[0].astype(BF16)

    acc = _dot(a_ref[...], bb_ref[...])
    if mode == "qkv":
        acc = acc * jnp.where(pl.program_id(0) < q_tiles, q_scale, 1.0).astype(F32)
    elif mode == "sigmoid":
        acc = _sigmoid(acc)
    o_ref[...] = acc.astype(o_ref.dtype)


def _proj(xb, w_in, layer, col0, width, out_dtype, mode, q_width=0):
    S, D = xb.shape
    tm = _tile(S, 1024, 8)
    tn = _tile(int(np.gcd.reduce([width, col0 or width, q_width or width])), 1024)
    assert col0 % tn == 0 and width % tn == 0 and q_width % tn == 0
    off = col0 // tn
    kern = functools.partial(_proj_kernel, mode=mode, q_tiles=q_width // tn,
                             q_scale=HEAD_DIM ** -0.5 * LOG2_E)
    return pl.pallas_call(
        kern, name="proj_" + mode,
        grid=(width // tn, S // tm),
        in_specs=[pl.BlockSpec((tm, D), lambda n, i: (i, 0)),
                  pl.BlockSpec((1, D, tn), lambda n, i: (layer, 0, off + n))],
        out_specs=pl.BlockSpec((tm, tn), lambda n, i: (i, n)),
        out_shape=jax.ShapeDtypeStruct((S, width), out_dtype),
        scratch_shapes=[pltpu.VMEM((D, tn), BF16)],
        compiler_params=_params("parallel", "arbitrary"),
    )(xb, w_in)


def _pool_kernel(u_ref, uprev_ref, pw_ref, sc_ref, o_ref, ext_ref, *, tp, group):
    i = pl.program_id(0)
    halo = uprev_ref[...]
    ext_ref[0:POOL_HALO, :] = jnp.where(i > 0, halo, jnp.zeros_like(halo))
    ext_ref[POOL_HALO:, :] = u_ref[...]
    t = i * tp + lax.broadcasted_iota(jnp.int32, (tp, 1), 0)
    for g, w in enumerate(POOL_WINDOWS):
        cols = slice(g * group, (g + 1) * group)
        s = ext_ref[:, cols]
        k = 1
        while k < w:
            s = s + pltpu.roll(s, k, 0)
            k *= 2
        cnt = jnp.minimum(t + 1, w).astype(F32)
        cur = u_ref[:, cols]
        pooled = s[POOL_HALO:, :] / cnt - cur
        y = _dot(pooled.astype(BF16), pw_ref[g]) * sc_ref[:, cols]
        o_ref[:, cols] = y.astype(o_ref.dtype)


def _pool(u, pool_w_b, pool_scale):
    S, PW = u.shape
    G, C, _ = pool_w_b.shape
    assert G == len(POOL_WINDOWS) and G * C == PW
    tp = _tile(S, 512, POOL_HALO)
    per = tp // POOL_HALO
    return pl.pallas_call(
        functools.partial(_pool_kernel, tp=tp, group=C), name="pool",
        grid=(S // tp,),
        in_specs=[pl.BlockSpec((tp, PW), lambda i: (i, 0)),
                  pl.BlockSpec((POOL_HALO, PW), lambda i: (jnp.maximum(i * per - 1, 0), 0)),
                  pl.BlockSpec((G, C, C), lambda i: (0, 0, 0)),
                  pl.BlockSpec((1, PW), lambda i: (0, 0))],
        out_specs=pl.BlockSpec((tp, PW), lambda i: (i, 0)),
        out_shape=jax.ShapeDtypeStruct((S, PW), BF16),
        scratch_shapes=[pltpu.VMEM((tp + POOL_HALO, PW), F32)],
        compiler_params=_params("parallel"),
    )(u, u, pool_w_b, pool_scale.reshape(1, PW))


def _moba_kernel(q_ref, k_ref, v_ref, o_ref, *scratch, nb, group, heads, qblocks):
    step = pl.program_id(1)
    for qb in range(qblocks):
        rows = pl.ds(qb * MOBA_BLOCK, MOBA_BLOCK)
        _moba_block(step * qblocks + qb, q_ref.at[rows, :], k_ref, v_ref, o_ref.at[rows, :], *scratch,
                    nb=nb, group=group, heads=heads)


def _moba_block(i, q_ref, k_ref, v_ref, o_ref, kmean_ref, vt_ref, bias_ref, sa_ref, sb_ref,
                *, nb, group, heads):
    B = MOBA_BLOCK
    neg = jnp.float32(-jnp.inf)
    U = group
    last = nb // U - 1
    lanes = [slice(h * HEAD_DIM, (h + 1) * HEAD_DIM) for h in range(heads)]

    @pl.when(i == 0)
    def _():
        def prep(g, carry):
            for h in range(heads):
                for u in range(U):
                    rj = pl.multiple_of((g * U + u) * B, B)
                    vt_ref[h, g, :, u * B:(u + 1) * B] = (
                        v_ref[pl.ds(rj, B), lanes[h]].astype(F32).T.astype(BF16))
                    kj = k_ref[pl.ds(rj, B), lanes[h]].astype(F32)
                    kmean_ref[h, pl.ds(g * U + u, 1), :] = jnp.sum(kj, axis=0, keepdims=True) / B
            return carry

        lax.fori_loop(0, nb // U, prep, 0)

    row0 = pl.multiple_of(i * B, B)
    blk = lax.broadcasted_iota(jnp.int32, (nb, B), 0)
    key_id = lax.broadcasted_iota(jnp.int32, (B, B), 0)
    qry_id = lax.broadcasted_iota(jnp.int32, (B, B), 1)
    qts = [q_ref[:, lanes[h]].astype(F32).T.astype(BF16) for h in range(heads)]

    def scores(h, g):
        r0 = pl.multiple_of(jnp.minimum(g, last) * (U * B), U * B)
        return _dot(k_ref[pl.ds(r0, U * B), lanes[h]], qts[h])

    gates, s_own, vt_own = [], [], []
    for h in range(heads):
        km = kmean_ref[h]
        km_hi = km.astype(BF16)
        km_lo = (km - km_hi.astype(F32)).astype(BF16)
        gates.append(_dot(km_hi, qts[h]) + _dot(km_lo, qts[h]))
        s_own.append(_dot(k_ref[pl.ds(row0, B), lanes[h]], qts[h]))
        vt_own.append(v_ref[pl.ds(row0, B), lanes[h]].astype(F32).T.astype(BF16))
    for h in range(heads):
        sa_ref[h] = scores(h, 0)

    init = []
    for h in range(heads):
        g = jnp.where(blk < i, gates[h], neg)
        bias = jnp.full((nb, B), neg, F32)
        for _ in range(MOBA_TOPK):
            m = jnp.max(g, axis=0, keepdims=True)
            idx = jnp.min(jnp.where(g == m, blk, nb), axis=0, keepdims=True)
            hit = blk == idx
            bias = jnp.where(hit & (m > neg), 0.0, bias)
            g = jnp.where(hit, neg, g)
        bias_ref[h, 0:nb, :] = bias
        bias_ref[h, nb:, :] = jnp.full((U, B), neg, F32)

        s = jnp.where(key_id <= qry_id, s_own[h], neg)
        m0 = jnp.max(s, axis=0, keepdims=True)
        p = jnp.exp2(s - m0)
        l0 = jnp.sum(p, axis=0, keepdims=True)
        init.append((m0, l0, p.astype(BF16)))
    init = [(m0, l0, _dot(vt_own[h], p)) for h, (m0, l0, p) in enumerate(init)]

    def process(h, s_ref, g, state):
        m_run, l_run, acc = state
        sel_u = [bias_ref[h, pl.ds(g * U + u, 1), :] for u in range(U)]
        s_u = [s_ref[h, u * B:(u + 1) * B, :] for u in range(U)]
        m_new = m_run
        for u in range(U):
            m_new = jnp.maximum(m_new, jnp.max(s_u[u], axis=0, keepdims=True) + sel_u[u])
        a = jnp.exp2(m_run - m_new)
        l_new = a * l_run
        acc_new = a * acc
        gv = jnp.minimum(g, last)
        for u in range(U):
            p = jnp.exp2(s_u[u] - (m_new - sel_u[u]))
            l_new = l_new + jnp.sum(p, axis=0, keepdims=True)
            acc_new = acc_new + _dot(vt_ref[h, gv, :, u * B:(u + 1) * B], p.astype(BF16))
        return m_new, l_new, acc_new

    def pair(t, states):
        for h in range(heads):
            sb_ref[h] = scores(h, 2 * t + 1)
        states = tuple(process(h, sa_ref, 2 * t, states[h]) for h in range(heads))
        for h in range(heads):
            sa_ref[h] = scores(h, 2 * t + 2)
        return tuple(process(h, sb_ref, 2 * t + 1, states[h]) for h in range(heads))

    n_groups = (i + U - 1) // U
    final = lax.fori_loop(0, n_groups // 2, pair, tuple(init))
    final = lax.cond(
        n_groups % 2 == 1,
        lambda st: tuple(process(h, sa_ref, n_groups - 1, st[h]) for h in range(heads)),
        lambda st: st,
        final)
    for h in range(heads):
        _, l_fin, acc_fin = final[h]
        o_ref[:, lanes[h]] = (acc_fin / l_fin).T.astype(o_ref.dtype)


def _moba(qkv, n_heads):
    S = qkv.shape[0]
    assert S % MOBA_BLOCK == 0
    nb = S // MOBA_BLOCK
    H = n_heads
    group = MOBA_GROUP
    hp = MOBA_HEADS_PER_STEP
    qbs = MOBA_QBLOCKS_PER_STEP
    assert nb % group == 0 and H % hp == 0 and nb % qbs == 0
    G = H // hp
    W = hp * HEAD_DIM
    return pl.pallas_call(
        functools.partial(_moba_kernel, nb=nb, group=group, heads=hp, qblocks=qbs), name="moba",
        grid=(G, nb // qbs),
        in_specs=[pl.BlockSpec((qbs * MOBA_BLOCK, W), lambda h, i: (i, h)),
                  pl.BlockSpec((S, W), lambda h, i: (0, G + h)),
                  pl.BlockSpec((S, W), lambda h, i: (0, 2 * G + h))],
        out_specs=pl.BlockSpec((qbs * MOBA_BLOCK, W), lambda h, i: (i, h)),
        out_shape=jax.ShapeDtypeStruct((S, H * HEAD_DIM), BF16),
        scratch_shapes=[pltpu.VMEM((hp, nb, HEAD_DIM), F32),
                        pltpu.VMEM((hp, nb // group, HEAD_DIM, group * MOBA_BLOCK), BF16),
                        pltpu.VMEM((hp, nb + group, MOBA_BLOCK), F32),
                        pltpu.VMEM((hp, group * MOBA_BLOCK, MOBA_BLOCK), F32),
                        pltpu.VMEM((hp, group * MOBA_BLOCK, MOBA_BLOCK), F32)],
        compiler_params=_params("parallel", "arbitrary"),
    )(qkv, qkv, qkv)


def _mix_up_kernel(p_ref, a_ref, wp_ref, wa_ref, gp_ref, ga_ref, o_ref):
    y_pool = _dot(p_ref[...], wp_ref[...])
    y_attn = _dot(a_ref[...], wa_ref[...])
    o_ref[...] = (gp_ref[...] * y_pool + ga_ref[...] * y_attn).astype(o_ref.dtype)


def _mix_up(p, o, w_up_pool_b, w_up_attn_b, gates):
    S, PW = p.shape
    AW = o.shape[1]
    D = w_up_pool_b.shape[1]
    tm = _tile(S, 1024, 8)
    tn = _tile(D, 1024)
    nt = D // tn
    return pl.pallas_call(
        _mix_up_kernel, name="mix_up",
        grid=(S // tm, nt),
        in_specs=[pl.BlockSpec((tm, PW), lambda i, n: (i, 0)),
                  pl.BlockSpec((tm, AW), lambda i, n: (i, 0)),
                  pl.BlockSpec((PW, tn), lambda i, n: (0, n)),
                  pl.BlockSpec((AW, tn), lambda i, n: (0, n)),
                  pl.BlockSpec((tm, tn), lambda i, n: (i, n)),
                  pl.BlockSpec((tm, tn), lambda i, n: (i, nt + n))],
        out_specs=pl.BlockSpec((tm, tn), lambda i, n: (i, n)),
        out_shape=jax.ShapeDtypeStruct((S, D), BF16),
        compiler_params=_params("parallel", "arbitrary"),
    )(p, o, w_up_pool_b, w_up_attn_b, gates, gates)


def _mix_out_kernel(m_ref, w_ref, x_ref, g_ref, b_ref, xf_ref, xb_ref, *, alpha):
    mix = _dot(m_ref[...], w_ref[...])
    y = _layer_norm(alpha * x_ref[...] + mix, g_ref[...], b_ref[...])
    xf_ref[...] = y
    xb_ref[...] = y.astype(BF16)


def _mix_out(m, w_o_b, x, g, b, alpha):
    S, D = x.shape
    tm = _tile(S, 512, 8)
    row = lambda dt: pl.BlockSpec((tm, D), lambda i: (i, 0))
    vec = pl.BlockSpec((1, D), lambda i: (0, 0))
    return pl.pallas_call(
        functools.partial(_mix_out_kernel, alpha=alpha), name="mix_out",
        grid=(S // tm,),
        in_specs=[row(BF16), pl.BlockSpec((D, D), lambda i: (0, 0)), row(F32), vec, vec],
        out_specs=[row(F32), row(BF16)],
        out_shape=[jax.ShapeDtypeStruct((S, D), F32), jax.ShapeDtypeStruct((S, D), BF16)],
        compiler_params=_params("parallel"),
    )(m, w_o_b, x, g.reshape(1, D), b.reshape(1, D))


def _swiglu_kernel(nvalid_ref, eid_ref, x_ref, wg_ref, wu_ref, wd_ref, o_ref, *, sub, tail):
    c = pl.program_id(0)
    f = pl.program_id(1)
    nv = nvalid_ref[c]
    rows = x_ref.shape[0]

    @pl.when(f == 0)
    def _():
        o_ref[...] = jnp.zeros(o_ref.shape, F32)

    def run(n_rows):
        wg = wg_ref[0].astype(BF16)
        wu = wu_ref[0].astype(BF16)
        wd = wd_ref[0].astype(BF16)
        for s0 in range(0, n_rows, sub):
            rs = slice(s0, min(s0 + sub, n_rows))
            xs = x_ref[rs, :]
            gate = _dot(xs, wg)
            up = _dot(xs, wu)
            h = (gate * _sigmoid(gate) * up).astype(BF16)
            o_ref[rs, :] += _dot(h, wd)

    n_var = rows // tail
    for k in range(1, n_var + 1):
        lo = (k - 1) * tail
        cond = (nv > lo) if k == n_var else ((nv > lo) & (nv <= lo + tail))
        pl.when(cond)(functools.partial(run, k * tail))


def _swiglu(xrows, w_gate, w_up, w_down, nvalid, eid, w_base, chunk, tf_pref):
    R, D = xrows.shape
    F = w_gate.shape[-1]
    tf = _tile(F, tf_pref)
    sub = min(EXPERT_SUB, chunk)
    tail = min(EXPERT_TAIL, sub)
    assert R % chunk == 0 and chunk % sub == 0 and sub % tail == 0 and tail % GATHER_ROWS == 0

    def w_in_map(c, f, nv, e):
        return (w_base + e[c], 0, jnp.where(nv[c] > 0, f, 0))

    def w_out_map(c, f, nv, e):
        return (w_base + e[c], jnp.where(nv[c] > 0, f, 0), 0)

    grid_spec = pltpu.PrefetchScalarGridSpec(
        num_scalar_prefetch=2,
        grid=(R // chunk, F // tf),
        in_specs=[pl.BlockSpec((chunk, D), lambda c, f, nv, e: (c, 0)),
                  pl.BlockSpec((1, D, tf), w_in_map),
                  pl.BlockSpec((1, D, tf), w_in_map),
                  pl.BlockSpec((1, tf, D), w_out_map)],
        out_specs=pl.BlockSpec((chunk, D), lambda c, f, nv, e: (c, 0)),
    )
    return pl.pallas_call(
        functools.partial(_swiglu_kernel, sub=sub, tail=tail), name="swiglu",
        grid_spec=grid_spec,
        out_shape=jax.ShapeDtypeStruct((R, D), F32),
        compiler_params=_params("parallel", "arbitrary"),
    )(nvalid, eid, xrows, w_gate, w_up, w_down)


def _router_kernel(x_ref, rh_ref, rl_ref, o_ref):
    x = x_ref[...]
    xh = x.astype(BF16)
    xl = (x - xh.astype(F32)).astype(BF16)
    o_ref[...] = _dot(xh, rh_ref[...]) + _dot(xl, rh_ref[...]) + _dot(xh, rl_ref[...])


def _router(x, router_w):
    S, D = x.shape
    E = router_w.shape[1]
    wpad = jnp.pad(router_w, ((0, 0), (0, LANES - E)))
    rh = wpad.astype(BF16)
    rl = (wpad - rh.astype(F32)).astype(BF16)
    tm = _tile(S, 512, 8)
    wspec = pl.BlockSpec((D, LANES), lambda i: (0, 0))
    logits = pl.pallas_call(
        _router_kernel, name="router",
        grid=(S // tm,),
        in_specs=[pl.BlockSpec((tm, D), lambda i: (i, 0)), wspec, wspec],
        out_specs=pl.BlockSpec((tm, LANES), lambda i: (i, 0)),
        out_shape=jax.ShapeDtypeStruct((S, LANES), F32),
        compiler_params=_params("parallel"),
    )(x, rh, rl)
    return logits[:, :E]


def _gather_kernel(tok_ref, valid_ref, x_hbm, o_ref, buf_ref, sem, *, rows, n_blocks):
    b = pl.program_id(0)

    def start(blk, slot):
        base = blk * rows

        def issue(r, carry):
            tok = tok_ref[base + r]
            pltpu.make_async_copy(x_hbm.at[pl.ds(tok, 1), :], buf_ref.at[slot, pl.ds(r, 1), :],
                                  sem.at[slot]).start()
            return carry

        lax.fori_loop(0, rows, issue, 0, unroll=DMA_ISSUE_UNROLL)

    @pl.when((b == 0) & (valid_ref[0] > 0))
    def _():
        start(0, 0)

    nxt = jnp.minimum(b + 1, n_blocks - 1)

    @pl.when((b + 1 < n_blocks) & (valid_ref[nxt] > 0))
    def _():
        start(nxt, nxt % 2)

    @pl.when(valid_ref[b] > 0)
    def _():
        slot = b % 2
        pltpu.make_async_copy(x_hbm.at[pl.ds(0, rows), :], buf_ref.at[slot], sem.at[slot]).wait()
        o_ref[...] = buf_ref[slot].astype(o_ref.dtype)

    @pl.when(valid_ref[b] == 0)
    def _():
        o_ref[...] = jnp.zeros(o_ref.shape, o_ref.dtype)


def _gather_rows(x, row_tok, blk_valid, n_rows):
    S, D = x.shape
    rows = GATHER_ROWS
    grid_spec = pltpu.PrefetchScalarGridSpec(
        num_scalar_prefetch=2,
        grid=(n_rows // rows,),
        in_specs=[pl.BlockSpec(memory_space=pl.ANY)],
        out_specs=pl.BlockSpec((rows, D), lambda b, t, v: (b, 0)),
        scratch_shapes=[pltpu.VMEM((2, rows, D), F32), pltpu.SemaphoreType.DMA((2,))],
    )
    return pl.pallas_call(
        functools.partial(_gather_kernel, rows=rows, n_blocks=n_rows // rows), name="gather_rows",
        grid_spec=grid_spec,
        out_shape=jax.ShapeDtypeStruct((n_rows, D), BF16),
        compiler_params=_params("arbitrary"),
    )(row_tok, blk_valid, x)


def _combine_kernel(pos_ref, y_hbm, w_ref, x_ref, g_ref, b_ref, xf_ref, xb_ref, buf_ref, sem,
                    *, rows, n_blocks, alpha):
    b = pl.program_id(0)

    def start(blk, slot):
        base = blk * rows * MOE_TOPK

        def issue(r, carry):
            for k in range(MOE_TOPK):
                src = pos_ref[base + r * MOE_TOPK + k]
                pltpu.make_async_copy(y_hbm.at[pl.ds(src, 1), :], buf_ref.at[slot, k, pl.ds(r, 1), :],
                                      sem.at[slot]).start()
            return carry

        lax.fori_loop(0, rows, issue, 0, unroll=DMA_ISSUE_UNROLL)

    @pl.when(b == 0)
    def _():
        start(0, 0)

    @pl.when(b + 1 < n_blocks)
    def _():
        start(b + 1, (b + 1) % 2)

    slot = b % 2
    for k in range(MOE_TOPK):
        pltpu.make_async_copy(y_hbm.at[pl.ds(0, rows), :], buf_ref.at[slot, k], sem.at[slot]).wait()
    w = w_ref[...]
    f = w[:, 0:1] * buf_ref[slot, 0] + w[:, 1:2] * buf_ref[slot, 1]
    y = _layer_norm(alpha * x_ref[...] + f, g_ref[...], b_ref[...])
    xf_ref[...] = y
    xb_ref[...] = y.astype(BF16)


def _combine_ln(y, pos, top_w, x, g, b, alpha):
    S, D = x.shape
    rows = _tile(S, COMBINE_ROWS, 8)
    row = pl.BlockSpec((rows, D), lambda i, p: (i, 0))
    vec = pl.BlockSpec((1, D), lambda i, p: (0, 0))
    grid_spec = pltpu.PrefetchScalarGridSpec(
        num_scalar_prefetch=1,
        grid=(S // rows,),
        in_specs=[pl.BlockSpec(memory_space=pl.ANY),
                  pl.BlockSpec((rows, MOE_TOPK), lambda i, p: (i, 0)),
                  row, vec, vec],
        out_specs=[row, row],
        scratch_shapes=[pltpu.VMEM((2, MOE_TOPK, rows, D), F32), pltpu.SemaphoreType.DMA((2,))],
    )
    return pl.pallas_call(
        functools.partial(_combine_kernel, rows=rows, n_blocks=S // rows, alpha=alpha), name="combine_ln",
        grid_spec=grid_spec,
        out_shape=[jax.ShapeDtypeStruct((S, D), F32), jax.ShapeDtypeStruct((S, D), BF16)],
        compiler_params=_params("arbitrary"),
    )(pos, y, top_w, x, g.reshape(1, D), b.reshape(1, D))


def _route(logits, chunk):
    S, E = logits.shape
    top_logit, top_e = lax.top_k(logits, MOE_TOPK)
    top_w = jax.nn.softmax(top_logit, axis=-1)
    n_assign = S * MOE_TOPK
    flat_e = top_e.reshape(-1).astype(jnp.int32)
    onehot = (flat_e[:, None] == jnp.arange(E, dtype=jnp.int32)[None, :]).astype(jnp.int32)
    running = jnp.cumsum(onehot, axis=0)
    rank = jnp.sum(running * onehot, axis=1) - 1
    counts = running[-1]
    padded = (counts + chunk - 1) // chunk * chunk
    ends = jnp.cumsum(padded)
    pstart = ends - padded
    dest = pstart[flat_e] + rank
    n_rows = (-(-n_assign // chunk) + E) * chunk
    n_chunks = n_rows // chunk
    flat_tok = jnp.arange(n_assign, dtype=jnp.int32) // MOE_TOPK
    chunk_start = jnp.arange(n_chunks, dtype=jnp.int32) * chunk
    eid = jnp.minimum(jnp.searchsorted(ends, chunk_start, side="right"), E - 1).astype(jnp.int32)
    nvalid = jnp.clip(counts[eid] - (chunk_start - pstart[eid]), 0, chunk).astype(jnp.int32)
    _, tok_sorted = lax.sort_key_val(flat_e, flat_tok)
    first = jnp.cumsum(counts) - counts
    row = jnp.arange(n_rows, dtype=jnp.int32)
    row_e = eid[row // chunk]
    row_k = row - pstart[row_e]
    src = jnp.clip(first[row_e] + row_k, 0, n_assign - 1)
    row_tok = jnp.where(row_k < counts[row_e], tok_sorted[src], 0).astype(jnp.int32)
    per = chunk // GATHER_ROWS
    blk = jnp.arange(n_rows // GATHER_ROWS, dtype=jnp.int32)
    blk_valid = (nvalid[blk // per] > (blk % per) * GATHER_ROWS).astype(jnp.int32)
    return top_w, dest.astype(jnp.int32), row_tok, eid, nvalid, blk_valid, n_rows


def kernel(x, ln_in_g, ln_in_b, w_in, pool_w, pool_scale, w_up_pool, w_up_attn, w_o, ln_mix_g, ln_mix_b, ffn_w_gate, ffn_w_up, ffn_w_down, moe_router, moe_w_gate, moe_w_up, moe_w_down, ln_ffn_g, ln_ffn_b):
    B, S, D = x.shape
    assert B == 1
    depth = w_in.shape[0]
    PW = w_up_pool.shape[1]
    AW = w_up_attn.shape[1]
    n_heads = AW // HEAD_DIM
    n_exp = moe_router.shape[-1]
    alpha = float((2 * depth) ** 0.25)

    pool_w_b = pool_w.astype(BF16)
    w_up_pool_b = w_up_pool.astype(BF16)
    w_up_attn_b = w_up_attn.astype(BF16)
    w_o_b = w_o.astype(BF16)
    moe_gate = moe_w_gate.reshape((-1,) + moe_w_gate.shape[2:])
    moe_up = moe_w_up.reshape((-1,) + moe_w_up.shape[2:])
    moe_down = moe_w_down.reshape((-1,) + moe_w_down.shape[2:])

    chunk = min(EXPERT_CHUNK, S)
    dense_nvalid = jnp.full((S // chunk,), chunk, jnp.int32)
    dense_eid = jnp.zeros((S // chunk,), jnp.int32)

    xf, xb = _ln_in(x.reshape(S, D), ln_in_g, ln_in_b)
    for l in range(depth):
        u = _proj(xb, w_in, l, 0, PW, F32, "plain")
        qkv = _proj(xb, w_in, l, PW, 3 * AW, BF16, "qkv", q_width=AW)
        gates = _proj(xb, w_in, l, PW + 3 * AW, 2 * D, BF16, "sigmoid")
        p = _pool(u, pool_w_b[l], pool_scale[l])
        o = _moba(qkv, n_heads)
        m = _mix_up(p, o, w_up_pool_b[l], w_up_attn_b[l], gates)
        xf, xb = _mix_out(m, w_o_b[l], xf, ln_mix_g[l], ln_mix_b[l], alpha)
        i = l // 2
        if l % 2 == 0:
            f = _swiglu(xb, ffn_w_gate, ffn_w_up, ffn_w_down, dense_nvalid, dense_eid, i, chunk, 512)
            xf, xb = _add_ln(xf, f, ln_ffn_g[l], ln_ffn_b[l], alpha)
        else:
            logits = _router(xf, moe_router[i])
            top_w, dest, row_tok, eid, nvalid, blk_valid, n_rows = _route(logits, chunk)
            xg = _gather_rows(xf, row_tok, blk_valid, n_rows)
            y = _swiglu(xg, moe_gate, moe_up, moe_down, nvalid, eid, i * n_exp, chunk, 512)
            xf, xb = _combine_ln(y, dest, top_w, xf, ln_ffn_g[l], ln_ffn_b[l], alpha)
    return xf.reshape(B, S, D)
```

```python
import functools

import jax
import jax.numpy as jnp
import numpy as np
from jax import lax
from jax.experimental import pallas as pl
from jax.experimental.pallas import tpu as pltpu

F32 = jnp.float32
BF16 = jnp.bfloat16

HEAD_DIM = 128
MOBA_BLOCK = 256
MOBA_TOPK = 3
MOBA_GROUP = 2
MOBA_SUM_ROWS = 16
MOBA_QBLOCKS_PER_STEP = 2
MOBA_HEADS_PER_STEP = 4
POOL_WINDOWS = (2, 4, 8, 16)
POOL_HALO = 16
MOE_TOPK = 2
LN_EPS = 1e-5
LOG2_E = float(np.log2(np.e))
LANES = 128
VMEM_LIMIT = 56 * 1024 * 1024

EXPERT_CHUNK = 1024
EXPERT_SUB = 512
EXPERT_TAIL = 256
GATHER_ROWS = 256
COMBINE_ROWS = 512
DMA_ISSUE_UNROLL = 8


def _tile(n, pref, mult=LANES):
    if n <= pref:
        return n
    t = (pref // mult) * mult
    while t > mult and n % t:
        t -= mult
    assert n % t == 0, (n, pref)
    return t


def _params(*sem):
    return pltpu.CompilerParams(dimension_semantics=sem, vmem_limit_bytes=VMEM_LIMIT)


def _layer_norm(v, g, b):
    mu = jnp.mean(v, axis=-1, keepdims=True)
    d = v - mu
    var = jnp.mean(d * d, axis=-1, keepdims=True)
    return d * lax.rsqrt(var + LN_EPS) * g + b


def _sigmoid(v):
    return 1.0 / (1.0 + jnp.exp(-v))


def _dot(a, b):
    return jnp.dot(a, b, preferred_element_type=F32)


def _dot_nt(a, b):
    return lax.dot_general(a, b, (((1,), (1,)), ((), ())), preferred_element_type=F32)


def _ln_kernel(x_ref, g_ref, b_ref, xf_ref, xb_ref):
    y = _layer_norm(x_ref[...], g_ref[...], b_ref[...])
    xf_ref[...] = y
    xb_ref[...] = y.astype(BF16)


def _ln_in(x, g, b):
    S, D = x.shape
    tm = _tile(S, 512, 8)
    row = pl.BlockSpec((tm, D), lambda i: (i, 0))
    vec = pl.BlockSpec((1, D), lambda i: (0, 0))
    return pl.pallas_call(
        _ln_kernel, name="ln_in",
        grid=(S // tm,),
        in_specs=[row, vec, vec],
        out_specs=[row, row],
        out_shape=[jax.ShapeDtypeStruct((S, D), F32), jax.ShapeDtypeStruct((S, D), BF16)],
        compiler_params=_params("parallel"),
    )(x, g.reshape(1, D), b.reshape(1, D))


def _add_ln_kernel(x_ref, f_ref, g_ref, b_ref, xf_ref, xb_ref, *, alpha):
    y = _layer_norm(alpha * x_ref[...] + f_ref[...], g_ref[...], b_ref[...])
    xf_ref[...] = y
    xb_ref[...] = y.astype(BF16)


def _add_ln(x, f, g, b, alpha):
    S, D = x.shape
    tm = _tile(S, 512, 8)
    row = pl.BlockSpec((tm, D), lambda i: (i, 0))
    vec = pl.BlockSpec((1, D), lambda i: (0, 0))
    return pl.pallas_call(
        functools.partial(_add_ln_kernel, alpha=alpha), name="add_ln",
        grid=(S // tm,),
        in_specs=[row, row, vec, vec],
        out_specs=[row, row],
        out_shape=[jax.ShapeDtypeStruct((S, D), F32), jax.ShapeDtypeStruct((S, D), BF16)],
        compiler_params=_params("parallel"),
    )(x, f, g.reshape(1, D), b.reshape(1, D))


def _proj_kernel(a_ref, b_ref, o_ref, bb_ref, *, mode, q_tiles, q_scale):
    @pl.when(pl.program_id(1) == 0)
    def _():
        bb_ref[...] = b_ref[0].astype(BF16)

    acc = _dot(a_ref[...], bb_ref[...])
    if mode == "qkv":
        acc = acc * jnp.where(pl.program_id(0) < q_tiles, q_scale, 1.0).astype(F32)
    elif mode == "sigmoid":
        acc = _sigmoid(acc)
    o_ref[...] = acc.astype(o_ref.dtype)


def _proj(xb, w_in, layer, col0, width, out_dtype, mode, q_width=0):
    S, D = xb.shape
    tm = _tile(S, 1024, 8)
    tn = _tile(int(np.gcd.reduce([width, col0 or width, q_width or width])), 1024)
    assert col0 % tn == 0 and width % tn == 0 and q_width % tn == 0
    off = col0 // tn
    kern = functools.partial(_proj_kernel, mode=mode, q_tiles=q_width // tn,
                             q_scale=HEAD_DIM ** -0.5 * LOG2_E)
    return pl.pallas_call(
        kern, name="proj_" + mode,
        grid=(width // tn, S // tm),
        in_specs=[pl.BlockSpec((tm, D), lambda n, i: (i, 0)),
                  pl.BlockSpec((1, D, tn), lambda n, i: (layer, 0, off + n))],
        out_specs=pl.BlockSpec((tm, tn), lambda n, i: (i, n)),
        out_shape=jax.ShapeDtypeStruct((S, width), out_dtype),
        scratch_shapes=[pltpu.VMEM((D, tn), BF16)],
        compiler_params=_params("parallel", "arbitrary"),
    )(xb, w_in)


def _pool_kernel(u_ref, uprev_ref, pw_ref, sc_ref, o_ref, ext_ref, *, tp, group):
    i = pl.program_id(0)
    halo = uprev_ref[...]
    ext_ref[0:POOL_HALO, :] = jnp.where(i > 0, halo, jnp.zeros_like(halo))
    ext_ref[POOL_HALO:, :] = u_ref[...]
    t = i * tp + lax.broadcasted_iota(jnp.int32, (tp, 1), 0)
    for g, w in enumerate(POOL_WINDOWS):
        cols = slice(g * group, (g + 1) * group)
        s = ext_ref[:, cols]
        k = 1
        while k < w:
            s = s + pltpu.roll(s, k, 0)
            k *= 2
        cnt = jnp.minimum(t + 1, w).astype(F32)
        cur = u_ref[:, cols]
        pooled = s[POOL_HALO:, :] / cnt - cur
        y = _dot(pooled.astype(BF16), pw_ref[g]) * sc_ref[:, cols]
        o_ref[:, cols] = y.astype(o_ref.dtype)


def _pool(u, pool_w_b, pool_scale):
    S, PW = u.shape
    G, C, _ = pool_w_b.shape
    assert G == len(POOL_WINDOWS) and G * C == PW
    tp = _tile(S, 512, POOL_HALO)
    per = tp // POOL_HALO
    return pl.pallas_call(
        functools.partial(_pool_kernel, tp=tp, group=C), name="pool",
        grid=(S // tp,),
        in_specs=[pl.BlockSpec((tp, PW), lambda i: (i, 0)),
                  pl.BlockSpec((POOL_HALO, PW), lambda i: (jnp.maximum(i * per - 1, 0), 0)),
                  pl.BlockSpec((G, C, C), lambda i: (0, 0, 0)),
                  pl.BlockSpec((1, PW), lambda i: (0, 0))],
        out_specs=pl.BlockSpec((tp, PW), lambda i: (i, 0)),
        out_shape=jax.ShapeDtypeStruct((S, PW), BF16),
        scratch_shapes=[pltpu.VMEM((tp + POOL_HALO, PW), F32)],
        compiler_params=_params("parallel"),
    )(u, u, pool_w_b, pool_scale.reshape(1, PW))


def _moba_kernel(q_ref, k_ref, v_ref, o_ref, *scratch, nb, group, heads, qblocks):
    step = pl.program_id(1)
    for qb in range(qblocks):
        rows = pl.ds(qb * MOBA_BLOCK, MOBA_BLOCK)
        _moba_block(step * qblocks + qb, q_ref.at[rows, :], k_ref, v_ref, o_ref.at[rows, :], *scratch,
                    nb=nb, group=group, heads=heads)


def _moba_block(i, q_ref, k_ref, v_ref, o_ref, kmean_ref, vt_ref, bias_ref, sa_ref, sb_ref,
                *, nb, group, heads):
    B = MOBA_BLOCK
    neg = jnp.float32(-jnp.inf)
    U = group
    last = nb // U - 1
    lanes = [slice(h * HEAD_DIM, (h + 1) * HEAD_DIM) for h in range(heads)]

    ones_rows = (lax.broadcasted_iota(jnp.int32, (MOBA_SUM_ROWS, B), 0) == 0).astype(BF16)

    @pl.when(i == 0)
    def _():
        def prep(g, carry):
            for h in range(heads):
                for u in range(U):
                    rj = pl.multiple_of((g * U + u) * B, B)
                    vt_ref[h, g, 0:HEAD_DIM, u * B:(u + 1) * B] = (
                        v_ref[pl.ds(rj, B), lanes[h]].astype(F32).T.astype(BF16))
                    vt_ref[h, g, HEAD_DIM:, u * B:(u + 1) * B] = ones_rows
                    kj = k_ref[pl.ds(rj, B), lanes[h]].astype(F32)
                    kmean_ref[h, pl.ds(g * U + u, 1), :] = jnp.sum(kj, axis=0, keepdims=True) / B
            return carry

        lax.fori_loop(0, nb // U, prep, 0)

    row0 = pl.multiple_of(i * B, B)
    blk = lax.broadcasted_iota(jnp.int32, (nb, B), 0)
    key_id = lax.broadcasted_iota(jnp.int32, (B, B), 0)
    qry_id = lax.broadcasted_iota(jnp.int32, (B, B), 1)
    qts = [q_ref[:, lanes[h]].astype(F32).T.astype(BF16) for h in range(heads)]

    def scores(h, g):
        r0 = pl.multiple_of(jnp.minimum(g, last) * (U * B), U * B)
        return _dot(k_ref[pl.ds(r0, U * B), lanes[h]], qts[h])

    gates, s_own, vt_own = [], [], []
    for h in range(heads):
        km = kmean_ref[h]
        km_hi = km.astype(BF16)
        km_lo = (km - km_hi.astype(F32)).astype(BF16)
        gates.append(_dot(km_hi, qts[h]) + _dot(km_lo, qts[h]))
        s_own.append(_dot(k_ref[pl.ds(row0, B), lanes[h]], qts[h]))
        vt_own.append(jnp.concatenate(
            [v_ref[pl.ds(row0, B), lanes[h]].astype(F32).T.astype(BF16), ones_rows], axis=0))
    for h in range(heads):
        sa_ref[h] = scores(h, 0)

    init = []
    for h in range(heads):
        g = jnp.where(blk < i, gates[h], neg)
        bias = jnp.full((nb, B), neg, F32)
        for _ in range(MOBA_TOPK):
            m = jnp.max(g, axis=0, keepdims=True)
            idx = jnp.min(jnp.where(g == m, blk, nb), axis=0, keepdims=True)
            hit = blk == idx
            bias = jnp.where(hit & (m > neg), 0.0, bias)
            g = jnp.where(hit, neg, g)
        bias_ref[h, 0:nb, :] = bias
        bias_ref[h, nb:, :] = jnp.full((U, B), neg, F32)

        s = jnp.where(key_id <= qry_id, s_own[h], neg)
        m0 = jnp.max(s, axis=0, keepdims=True)
        p = jnp.exp2(s - m0)
        init.append((m0, p.astype(BF16)))
    init = [(m0, _dot(vt_own[h], p)) for h, (m0, p) in enumerate(init)]

    def process(h, s_ref, g, state):
        m_run, acc = state
        sel_u = [bias_ref[h, pl.ds(g * U + u, 1), :] for u in range(U)]
        s_u = [s_ref[h, u * B:(u + 1) * B, :] for u in range(U)]
        m_new = m_run
        for u in range(U):
            m_new = jnp.maximum(m_new, jnp.max(s_u[u], axis=0, keepdims=True) + sel_u[u])
        a = jnp.exp2(m_run - m_new)
        acc_new = a * acc
        gv = jnp.minimum(g, last)
        for u in range(U):
            p = jnp.exp2(s_u[u] - (m_new - sel_u[u]))
            acc_new = acc_new + _dot(vt_ref[h, gv, :, u * B:(u + 1) * B], p.astype(BF16))
        return m_new, acc_new

    def pair(t, states):
        for h in range(heads):
            sb_ref[h] = scores(h, 2 * t + 1)
        states = tuple(process(h, sa_ref, 2 * t, states[h]) for h in range(heads))
        for h in range(heads):
            sa_ref[h] = scores(h, 2 * t + 2)
        return tuple(process(h, sb_ref, 2 * t + 1, states[h]) for h in range(heads))

    n_groups = (i + U - 1) // U
    final = lax.fori_loop(0, n_groups // 2, pair, tuple(init))
    final = lax.cond(
        n_groups % 2 == 1,
        lambda st: tuple(process(h, sa_ref, n_groups - 1, st[h]) for h in range(heads)),
        lambda st: st,
        final)
    for h in range(heads):
        _, acc_fin = final[h]
        o_fin = acc_fin[0:HEAD_DIM] / acc_fin[HEAD_DIM:HEAD_DIM + 1]
        o_ref[:, lanes[h]] = o_fin.T.astype(o_ref.dtype)


def _moba(qkv, n_heads):
    S = qkv.shape[0]
    assert S % MOBA_BLOCK == 0
    nb = S // MOBA_BLOCK
    H = n_heads
    group = MOBA_GROUP
    hp = MOBA_HEADS_PER_STEP
    qbs = MOBA_QBLOCKS_PER_STEP
    assert nb % group == 0 and H % hp == 0 and nb % qbs == 0
    G = H // hp
    W = hp * HEAD_DIM
    return pl.pallas_call(
        functools.partial(_moba_kernel, nb=nb, group=group, heads=hp, qblocks=qbs), name="moba",
        grid=(G, nb // qbs),
        in_specs=[pl.BlockSpec((qbs * MOBA_BLOCK, W), lambda h, i: (i, h)),
                  pl.BlockSpec((S, W), lambda h, i: (0, G + h)),
                  pl.BlockSpec((S, W), lambda h, i: (0, 2 * G + h))],
        out_specs=pl.BlockSpec((qbs * MOBA_BLOCK, W), lambda h, i: (i, h)),
        out_shape=jax.ShapeDtypeStruct((S, H * HEAD_DIM), BF16),
        scratch_shapes=[pltpu.VMEM((hp, nb, HEAD_DIM), F32),
                        pltpu.VMEM((hp, nb // group, HEAD_DIM + MOBA_SUM_ROWS, group * MOBA_BLOCK), BF16),
                        pltpu.VMEM((hp, nb + group, MOBA_BLOCK), F32),
                        pltpu.VMEM((hp, group * MOBA_BLOCK, MOBA_BLOCK), F32),
                        pltpu.VMEM((hp, group * MOBA_BLOCK, MOBA_BLOCK), F32)],
        compiler_params=_params("parallel", "arbitrary"),
    )(qkv, qkv, qkv)


def _mix_up_kernel(p_ref, a_ref, wp_ref, wa_ref, gp_ref, ga_ref, o_ref):
    y_pool = _dot(p_ref[...], wp_ref[...])
    y_attn = _dot(a_ref[...], wa_ref[...])
    o_ref[...] = (gp_ref[...] * y_pool + ga_ref[...] * y_attn).astype(o_ref.dtype)


def _mix_up(p, o, w_up_pool_b, w_up_attn_b, gates):
    S, PW = p.shape
    AW = o.shape[1]
    D = w_up_pool_b.shape[1]
    tm = _tile(S, 1024, 8)
    tn = _tile(D, 1024)
    nt = D // tn
    return pl.pallas_call(
        _mix_up_kernel, name="mix_up",
        grid=(S // tm, nt),
        in_specs=[pl.BlockSpec((tm, PW), lambda i, n: (i, 0)),
                  pl.BlockSpec((tm, AW), lambda i, n: (i, 0)),
                  pl.BlockSpec((PW, tn), lambda i, n: (0, n)),
                  pl.BlockSpec((AW, tn), lambda i, n: (0, n)),
                  pl.BlockSpec((tm, tn), lambda i, n: (i, n)),
                  pl.BlockSpec((tm, tn), lambda i, n: (i, nt + n))],
        out_specs=pl.BlockSpec((tm, tn), lambda i, n: (i, n)),
        out_shape=jax.ShapeDtypeStruct((S, D), BF16),
        compiler_params=_params("parallel", "arbitrary"),
    )(p, o, w_up_pool_b, w_up_attn_b, gates, gates)


def _mix_out_kernel(m_ref, w_ref, x_ref, g_ref, b_ref, xf_ref, xb_ref, *, alpha):
    mix = _dot(m_ref[...], w_ref[...])
    y = _layer_norm(alpha * x_ref[...] + mix, g_ref[...], b_ref[...])
    xf_ref[...] = y
    xb_ref[...] = y.astype(BF16)


def _mix_out(m, w_o_b, x, g, b, alpha):
    S, D = x.shape
    tm = _tile(S, 512, 8)
    row = lambda dt: pl.BlockSpec((tm, D), lambda i: (i, 0))
    vec = pl.BlockSpec((1, D), lambda i: (0, 0))
    return pl.pallas_call(
        functools.partial(_mix_out_kernel, alpha=alpha), name="mix_out",
        grid=(S // tm,),
        in_specs=[row(BF16), pl.BlockSpec((D, D), lambda i: (0, 0)), row(F32), vec, vec],
        out_specs=[row(F32), row(BF16)],
        out_shape=[jax.ShapeDtypeStruct((S, D), F32), jax.ShapeDtypeStruct((S, D), BF16)],
        compiler_params=_params("parallel"),
    )(m, w_o_b, x, g.reshape(1, D), b.reshape(1, D))


def _swiglu_kernel(nvalid_ref, eid_ref, x_ref, wg_ref, wu_ref, wd_ref, o_ref, *, sub, tail):
    c = pl.program_id(0)
    f = pl.program_id(1)
    nv = nvalid_ref[c]
    rows = x_ref.shape[0]

    @pl.when(f == 0)
    def _():
        o_ref[...] = jnp.zeros(o_ref.shape, F32)

    def run(n_rows):
        wg = wg_ref[0].astype(BF16)
        wu = wu_ref[0].astype(BF16)
        wd = wd_ref[0].astype(BF16)
        for s0 in range(0, n_rows, sub):
            rs = slice(s0, min(s0 + sub, n_rows))
            xs = x_ref[rs, :]
            gate = _dot(xs, wg)
            up = _dot(xs, wu)
            h = (gate * _sigmoid(gate) * up).astype(BF16)
            o_ref[rs, :] += _dot(h, wd)

    n_var = rows // tail
    for k in range(1, n_var + 1):
        lo = (k - 1) * tail
        cond = (nv > lo) if k == n_var else ((nv > lo) & (nv <= lo + tail))
        pl.when(cond)(functools.partial(run, k * tail))


def _swiglu(xrows, w_gate, w_up, w_down, nvalid, eid, w_base, chunk, tf_pref):
    R, D = xrows.shape
    F = w_gate.shape[-1]
    tf = _tile(F, tf_pref)
    sub = min(EXPERT_SUB, chunk)
    tail = min(EXPERT_TAIL, sub)
    assert R % chunk == 0 and chunk % sub == 0 and sub % tail == 0 and tail % GATHER_ROWS == 0

    def w_in_map(c, f, nv, e):
        return (w_base + e[c], 0, jnp.where(nv[c] > 0, f, 0))

    def w_out_map(c, f, nv, e):
        return (w_base + e[c], jnp.where(nv[c] > 0, f, 0), 0)

    grid_spec = pltpu.PrefetchScalarGridSpec(
        num_scalar_prefetch=2,
        grid=(R // chunk, F // tf),
        in_specs=[pl.BlockSpec((chunk, D), lambda c, f, nv, e: (c, 0)),
                  pl.BlockSpec((1, D, tf), w_in_map),
                  pl.BlockSpec((1, D, tf), w_in_map),
                  pl.BlockSpec((1, tf, D), w_out_map)],
        out_specs=pl.BlockSpec((chunk, D), lambda c, f, nv, e: (c, 0)),
    )
    return pl.pallas_call(
        functools.partial(_swiglu_kernel, sub=sub, tail=tail), name="swiglu",
        grid_spec=grid_spec,
        out_shape=jax.ShapeDtypeStruct((R, D), F32),
        compiler_params=_params("parallel", "arbitrary"),
    )(nvalid, eid, xrows, w_gate, w_up, w_down)


def _router_kernel(x_ref, rh_ref, rl_ref, o_ref):
    x = x_ref[...]
    xh = x.astype(BF16)
    xl = (x - xh.astype(F32)).astype(BF16)
    o_ref[...] = _dot(xh, rh_ref[...]) + _dot(xl, rh_ref[...]) + _dot(xh, rl_ref[...])


def _router(x, router_w):
    S, D = x.shape
    E = router_w.shape[1]
    wpad = jnp.pad(router_w, ((0, 0), (0, LANES - E)))
    rh = wpad.astype(BF16)
    rl = (wpad - rh.astype(F32)).astype(BF16)
    tm = _tile(S, 512, 8)
    wspec = pl.BlockSpec((D, LANES), lambda i: (0, 0))
    logits = pl.pallas_call(
        _router_kernel, name="router",
        grid=(S // tm,),
        in_specs=[pl.BlockSpec((tm, D), lambda i: (i, 0)), wspec, wspec],
        out_specs=pl.BlockSpec((tm, LANES), lambda i: (i, 0)),
        out_shape=jax.ShapeDtypeStruct((S, LANES), F32),
        compiler_params=_params("parallel"),
    )(x, rh, rl)
    return logits[:, :E]


def _gather_kernel(tok_ref, valid_ref, x_hbm, o_ref, buf_ref, sem, *, rows, n_blocks):
    b = pl.program_id(0)

    def start(blk, slot):
        base = blk * rows

        def issue(r, carry):
            tok = tok_ref[base + r]
            pltpu.make_async_copy(x_hbm.at[pl.ds(tok, 1), :], buf_ref.at[slot, pl.ds(r, 1), :],
                                  sem.at[slot]).start()
            return carry

        lax.fori_loop(0, rows, issue, 0, unroll=DMA_ISSUE_UNROLL)

    @pl.when((b == 0) & (valid_ref[0] > 0))
    def _():
        start(0, 0)

    nxt = jnp.minimum(b + 1, n_blocks - 1)

    @pl.when((b + 1 < n_blocks) & (valid_ref[nxt] > 0))
    def _():
        start(nxt, nxt % 2)

    @pl.when(valid_ref[b] > 0)
    def _():
        slot = b % 2
        pltpu.make_async_copy(x_hbm.at[pl.ds(0, rows), :], buf_ref.at[slot], sem.at[slot]).wait()
        o_ref[...] = buf_ref[slot].astype(o_ref.dtype)

    @pl.when(valid_ref[b] == 0)
    def _():
        o_ref[...] = jnp.zeros(o_ref.shape, o_ref.dtype)


def _gather_rows(x, row_tok, blk_valid, n_rows):
    S, D = x.shape
    rows = GATHER_ROWS
    grid_spec = pltpu.PrefetchScalarGridSpec(
        num_scalar_prefetch=2,
        grid=(n_rows // rows,),
        in_specs=[pl.BlockSpec(memory_space=pl.ANY)],
        out_specs=pl.BlockSpec((rows, D), lambda b, t, v: (b, 0)),
        scratch_shapes=[pltpu.VMEM((2, rows, D), F32), pltpu.SemaphoreType.DMA((2,))],
    )
    return pl.pallas_call(
        functools.partial(_gather_kernel, rows=rows, n_blocks=n_rows // rows), name="gather_rows",
        grid_spec=grid_spec,
        out_shape=jax.ShapeDtypeStruct((n_rows, D), BF16),
        compiler_params=_params("arbitrary"),
    )(row_tok, blk_valid, x)


def _combine_kernel(pos_ref, y_hbm, w_ref, x_ref, g_ref, b_ref, xf_ref, xb_ref, buf_ref, sem,
                    *, rows, n_blocks, alpha):
    b = pl.program_id(0)

    def start(blk, slot):
        base = blk * rows * MOE_TOPK

        def issue(r, carry):
            for k in range(MOE_TOPK):
                src = pos_ref[base + r * MOE_TOPK + k]
                pltpu.make_async_copy(y_hbm.at[pl.ds(src, 1), :], buf_ref.at[slot, k, pl.ds(r, 1), :],
                                      sem.at[slot]).start()
            return carry

        lax.fori_loop(0, rows, issue, 0, unroll=DMA_ISSUE_UNROLL)

    @pl.when(b == 0)
    def _():
        start(0, 0)

    @pl.when(b + 1 < n_blocks)
    def _():
        start(b + 1, (b + 1) % 2)

    slot = b % 2
    for k in range(MOE_TOPK):
        pltpu.make_async_copy(y_hbm.at[pl.ds(0, rows), :], buf_ref.at[slot, k], sem.at[slot]).wait()
    w = w_ref[...]
    f = w[:, 0:1] * buf_ref[slot, 0] + w[:, 1:2] * buf_ref[slot, 1]
    y = _layer_norm(alpha * x_ref[...] + f, g_ref[...], b_ref[...])
    xf_ref[...] = y
    xb_ref[...] = y.astype(BF16)


def _combine_ln(y, pos, top_w, x, g, b, alpha):
    S, D = x.shape
    rows = _tile(S, COMBINE_ROWS, 8)
    row = pl.BlockSpec((rows, D), lambda i, p: (i, 0))
    vec = pl.BlockSpec((1, D), lambda i, p: (0, 0))
    grid_spec = pltpu.PrefetchScalarGridSpec(
        num_scalar_prefetch=1,
        grid=(S // rows,),
        in_specs=[pl.BlockSpec(memory_space=pl.ANY),
                  pl.BlockSpec((rows, MOE_TOPK), lambda i, p: (i, 0)),
                  row, vec, vec],
        out_specs=[row, row],
        scratch_shapes=[pltpu.VMEM((2, MOE_TOPK, rows, D), F32), pltpu.SemaphoreType.DMA((2,))],
    )
    return pl.pallas_call(
        functools.partial(_combine_kernel, rows=rows, n_blocks=S // rows, alpha=alpha), name="combine_ln",
        grid_spec=grid_spec,
        out_shape=[jax.ShapeDtypeStruct((S, D), F32), jax.ShapeDtypeStruct((S, D), BF16)],
        compiler_params=_params("arbitrary"),
    )(pos, y, top_w, x, g.reshape(1, D), b.reshape(1, D))


def _route(logits, chunk):
    S, E = logits.shape
    top_logit, top_e = lax.top_k(logits, MOE_TOPK)
    top_w = jax.nn.softmax(top_logit, axis=-1)
    n_assign = S * MOE_TOPK
    flat_e = top_e.reshape(-1).astype(jnp.int32)
    onehot = (flat_e[:, None] == jnp.arange(E, dtype=jnp.int32)[None, :]).astype(jnp.int32)
    running = jnp.cumsum(onehot, axis=0)
    rank = jnp.sum(running * onehot, axis=1) - 1
    counts = running[-1]
    padded = (counts + chunk - 1) // chunk * chunk
    ends = jnp.cumsum(padded)
    pstart = ends - padded
    dest = pstart[flat_e] + rank
    n_rows = (-(-n_assign // chunk) + E) * chunk
    n_chunks = n_rows // chunk
    flat_tok = jnp.arange(n_assign, dtype=jnp.int32) // MOE_TOPK
    row_tok = jnp.zeros((n_rows,), jnp.int32).at[dest].set(flat_tok)
    chunk_start = jnp.arange(n_chunks, dtype=jnp.int32) * chunk
    eid = jnp.minimum(jnp.searchsorted(ends, chunk_start, side="right"), E - 1).astype(jnp.int32)
    nvalid = jnp.clip(counts[eid] - (chunk_start - pstart[eid]), 0, chunk).astype(jnp.int32)
    per = chunk // GATHER_ROWS
    blk = jnp.arange(n_rows // GATHER_ROWS, dtype=jnp.int32)
    blk_valid = (nvalid[blk // per] > (blk % per) * GATHER_ROWS).astype(jnp.int32)
    return top_w, dest.astype(jnp.int32), row_tok, eid, nvalid, blk_valid, n_rows


def kernel(x, ln_in_g, ln_in_b, w_in, pool_w, pool_scale, w_up_pool, w_up_attn, w_o, ln_mix_g, ln_mix_b, ffn_w_gate, ffn_w_up, ffn_w_down, moe_router, moe_w_gate, moe_w_up, moe_w_down, ln_ffn_g, ln_ffn_b):
    B, S, D = x.shape
    assert B == 1
    depth = w_in.shape[0]
    PW = w_up_pool.shape[1]
    AW = w_up_attn.shape[1]
    n_heads = AW // HEAD_DIM
    n_exp = moe_router.shape[-1]
    alpha = float((2 * depth) ** 0.25)

    pool_w_b = pool_w.astype(BF16)
    w_up_pool_b = w_up_pool.astype(BF16)
    w_up_attn_b = w_up_attn.astype(BF16)
    w_o_b = w_o.astype(BF16)
    moe_gate = moe_w_gate.reshape((-1,) + moe_w_gate.shape[2:])
    moe_up = moe_w_up.reshape((-1,) + moe_w_up.shape[2:])
    moe_down = moe_w_down.reshape((-1,) + moe_w_down.shape[2:])

    chunk = min(EXPERT_CHUNK, S)
    dense_nvalid = jnp.full((S // chunk,), chunk, jnp.int32)
    dense_eid = jnp.zeros((S // chunk,), jnp.int32)

    xf, xb = _ln_in(x.reshape(S, D), ln_in_g, ln_in_b)
    for l in range(depth):
        u = _proj(xb, w_in, l, 0, PW, F32, "plain")
        qkv = _proj(xb, w_in, l, PW, 3 * AW, BF16, "qkv", q_width=AW)
        gates = _proj(xb, w_in, l, PW + 3 * AW, 2 * D, BF16, "sigmoid")
        p = _pool(u, pool_w_b[l], pool_scale[l])
        o = _moba(qkv, n_heads)
        m = _mix_up(p, o, w_up_pool_b[l], w_up_attn_b[l], gates)
        xf, xb = _mix_out(m, w_o_b[l], xf, ln_mix_g[l], ln_mix_b[l], alpha)
        i = l // 2
        if l % 2 == 0:
            f = _swiglu(xb, ffn_w_gate, ffn_w_up, ffn_w_down, dense_nvalid, dense_eid, i, chunk, 512)
            xf, xb = _add_ln(xf, f, ln_ffn_g[l], ln_ffn_b[l], alpha)
        else:
            logits = _router(xf, moe_router[i])
            top_w, dest, row_tok, eid, nvalid, blk_valid, n_rows = _route(logits, chunk)
            xg = _gather_rows(xf, row_tok, blk_valid, n_rows)
            y = _swiglu(xg, moe_gate, moe_up, moe_down, nvalid, eid, i * n_exp, chunk, 512)
            xf, xb = _combine_ln(y, dest, top_w, xf, ln_ffn_g[l], ln_ffn_b[l], alpha)
    return xf.reshape(B, S, D)
```
